```python
import jax, jax.numpy as jnp
from jax import lax
import numpy as np

D_MODEL = 4096
BATCH = 2
SEQ = 8192
DEPTH = 2

NORM_EPS = 1e-6
ATTN_HEADS = 8
ATTN_HEAD_DIM = 128
ATTN_WIDTH = ATTN_HEADS * ATTN_HEAD_DIM
ATTN_PATTERNS = ((128, 1), (512, 4), (2048, 16))
ROPE_THETA = 10000.0
HGRN_HEADS = 8
HGRN_KEY_DIM = 128
HGRN_VAL_DIM = 128
HGRN_KEY_WIDTH = HGRN_HEADS * HGRN_KEY_DIM
HGRN_WIDTH = HGRN_HEADS * HGRN_VAL_DIM
HGRN_CHUNK = 64
SSM_HEADS = 32
SSM_HEAD_DIM = 64
SSM_WIDTH = SSM_HEADS * SSM_HEAD_DIM
SSM_GROUPS = 8
SSM_STATE = 128
SSM_CONV = 4
SSM_CHUNK = 128
SSM_CONV_CH = SSM_WIDTH + 2 * SSM_GROUPS * SSM_STATE

MIX_WIDTH = ATTN_WIDTH + HGRN_WIDTH + SSM_WIDTH
IN_SPLITS = (ATTN_WIDTH, ATTN_WIDTH, ATTN_WIDTH, ATTN_WIDTH,
             HGRN_KEY_WIDTH, HGRN_KEY_WIDTH, HGRN_WIDTH, HGRN_WIDTH,
             SSM_WIDTH, SSM_CONV_CH, SSM_HEADS)
IN_WIDTH = sum(IN_SPLITS)

kernel_name = "hymba_hgrn2_ssd_dilated_swa"


def rms_norm(x, w):
    xf = x.astype(jnp.float32)
    y = xf * lax.rsqrt(jnp.mean(xf * xf, axis=-1, keepdims=True) + NORM_EPS)
    return (y * w.astype(jnp.float32)).astype(x.dtype)


def apply_rope(t, positions):
    half = t.shape[-1] // 2
    inv_freq = ROPE_THETA ** (-jnp.arange(half, dtype=jnp.float32) / half)
    ang = positions.astype(jnp.float32)[..., None] * inv_freq
    cos = jnp.cos(ang)[:, :, None, :]
    sin = jnp.sin(ang)[:, :, None, :]
    t1, t2 = t[..., :half], t[..., half:]
    return jnp.concatenate([t1 * cos - t2 * sin, t2 * cos + t1 * sin], axis=-1)


def dilated_window_attention(q, k, v, window, dilation):
    B, S, H, Dh = q.shape
    blk = window // dilation
    L = S // dilation
    nblk = -(-L // blk)
    Lp = nblk * blk

    def to_sub(t):
        t = t.reshape(B, L, dilation, H, Dh).transpose(0, 2, 3, 1, 4)
        t = jnp.pad(t, ((0, 0), (0, 0), (0, 0), (0, Lp - L), (0, 0)))
        return t.reshape(B, dilation, H, nblk, blk, Dh)

    def with_prev(t):
        prev = jnp.pad(t, ((0, 0), (0, 0), (0, 0), (1, 0), (0, 0), (0, 0)))[:, :, :, :-1]
        return jnp.concatenate([prev, t], axis=-2)

    qb = to_sub(q)
    kc = with_prev(to_sub(k))
    vc = with_prev(to_sub(v))
    s = jnp.einsum('brhnqe,brhnke->brhnqk', qb, kc)
    qi = jnp.arange(blk)[:, None]
    kj = jnp.arange(2 * blk)[None, :]
    dist = qi - kj + blk
    bidx = jnp.arange(nblk)[:, None, None]
    valid = (dist >= 0) & (dist <= blk) & ((bidx > 0) | (kj >= blk))
    s = jnp.where(valid, s, -jnp.inf)
    m = jnp.max(s, axis=-1, keepdims=True)
    p = jnp.exp(s - m)
    den = jnp.sum(p, axis=-1)
    o = jnp.einsum('brhnqk,brhnke->brhnqe', p, vc) / den[..., None]
    lse = m[..., 0] + jnp.log(den)
    o = o.reshape(B, dilation, H, Lp, Dh)[:, :, :, :L].transpose(0, 3, 1, 2, 4).reshape(B, S, H, Dh)
    lse = lse.reshape(B, dilation, H, Lp)[..., :L].transpose(0, 3, 1, 2).reshape(B, S, H)
    return o, lse


def attention_mixer(q, k, v, gate, positions):
    B, S, _ = q.shape
    shp = (B, S, ATTN_HEADS, ATTN_HEAD_DIM)
    q = apply_rope(q.astype(jnp.float32).reshape(shp), positions) * (ATTN_HEAD_DIM ** -0.5)
    k = apply_rope(k.astype(jnp.float32).reshape(shp), positions)
    v = v.astype(jnp.float32).reshape(shp)
    outs, lses = [], []
    for window, dilation in ATTN_PATTERNS:
        o_g, lse_g = dilated_window_attention(q, k, v, window, dilation)
        outs.append(o_g)
        lses.append(lse_g)
    wts = jax.nn.softmax(jnp.stack(lses, axis=0), axis=0)
    o = jnp.sum(wts[..., None] * jnp.stack(outs, axis=0), axis=0)
    return o.reshape(B, S, ATTN_WIDTH) * jax.nn.silu(gate.astype(jnp.float32))


def hgrn2_mixer(q, f_raw, i, gate, lb, norm_w):
    B, S, _ = q.shape
    H, K, V, C = HGRN_HEADS, HGRN_KEY_DIM, HGRN_VAL_DIM, HGRN_CHUNK
    nc = S // C
    q = q.astype(jnp.float32).reshape(B, S, H, K)
    f_raw = f_raw.astype(jnp.float32).reshape(B, S, H, K)
    v = i.astype(jnp.float32).reshape(B, S, H, V)
    lb = lb.reshape(H, K)
    log_f = jnp.logaddexp(jnp.log(lb), jnp.log1p(-lb) + jax.nn.log_sigmoid(f_raw))
    key = (1.0 - lb) * jax.nn.sigmoid(-f_raw)

    def chunk(t):
        return t.reshape(B, nc, C, H, t.shape[-1]).transpose(1, 0, 3, 2, 4)

    causal = jnp.tril(jnp.ones((C, C), dtype=bool))

    def step(state, inp):
        qc, kc, vc, gc = inp
        b = jnp.cumsum(gc, axis=-2)
        diff = b[:, :, :, None, :] - b[:, :, None, :, :]
        decay = jnp.exp(jnp.where(causal[:, :, None], diff, -jnp.inf))
        a = jnp.einsum('bhtsk,bhsk->bhts', decay * qc[:, :, :, None, :], kc)
        o = jnp.einsum('bhts,bhsv->bhtv', a, vc) + jnp.einsum('bhtk,bhkv->bhtv', qc * jnp.exp(b), state)
        b_last = b[:, :, -1:, :]
        new_state = jnp.exp(b_last[:, :, 0, :])[..., None] * state + \
            jnp.einsum('bhsk,bhsv->bhkv', kc * jnp.exp(b_last - b), vc)
        return new_state, o

    s0 = jnp.zeros((B, H, K, V), jnp.float32)
    _, o = lax.scan(step, s0, (chunk(q), chunk(key), chunk(v), chunk(log_f)))
    o = o.transpose(1, 0, 3, 2, 4).reshape(B, S, H, V)
    o = o * lax.rsqrt(jnp.mean(o * o, axis=-1, keepdims=True) + NORM_EPS)
    o = o.reshape(B, S, HGRN_WIDTH) * norm_w.astype(jnp.float32)
    return o * jax.nn.silu(gate.astype(jnp.float32))


def segsum(x):
    T = x.shape[-1]
    xe = jnp.broadcast_to(x[..., None], x.shape + (T,))
    xe = jnp.where(jnp.tril(jnp.ones((T, T), dtype=bool), k=-1), xe, 0.0)
    cs = jnp.cumsum(xe, axis=-2)
    return jnp.where(jnp.tril(jnp.ones((T, T), dtype=bool)), cs, -jnp.inf)


def ssd_mixer(z, xbc, dt_raw, conv_w, conv_b, dt_bias, a_log, d_skip, norm_w):
    B, S, _ = z.shape
    G, E, P, N, Lc = SSM_GROUPS, SSM_HEADS // SSM_GROUPS, SSM_HEAD_DIM, SSM_STATE, SSM_CHUNK
    nc = S // Lc
    xbc = lax.conv_general_dilated(
        xbc.astype(jnp.float32), conv_w.astype(jnp.float32)[:, None, :],
        window_strides=(1,), padding=[(SSM_CONV - 1, 0)],
        dimension_numbers=('NWC', 'WIO', 'NWC'), feature_group_count=SSM_CONV_CH)
    xbc = jax.nn.silu(xbc + conv_b.astype(jnp.float32))
    xs, bm, cm = jnp.split(xbc, [SSM_WIDTH, SSM_WIDTH + G * N], axis=-1)
    dt = jax.nn.softplus(dt_raw.astype(jnp.float32) + dt_bias.astype(jnp.float32))
    a = -jnp.exp(a_log.astype(jnp.float32))

    xh = xs.reshape(B, nc, Lc, G, E, P)
    X = xh * dt.reshape(B, nc, Lc, G, E)[..., None]
    a_dt = (dt * a).reshape(B, nc, Lc, G, E).transpose(0, 3, 4, 1, 2)
    bc = bm.reshape(B, nc, Lc, G, N)
    cc = cm.reshape(B, nc, Lc, G, N)
    a_cum = jnp.cumsum(a_dt, axis=-1)
    lmat = jnp.exp(segsum(a_dt))
    cb = jnp.einsum('bclgn,bcsgn->bgcls', cc, bc)
    y_diag = jnp.einsum('bgecls,bcsgep->bclgep', cb[:, :, None] * lmat, X)
    decay_states = jnp.exp(a_cum[..., -1:] - a_cum)
    states = jnp.einsum('bcsgn,bgecs,bcsgep->bcgepn', bc, decay_states, X)
    states = jnp.concatenate([jnp.zeros_like(states[:, :1]), states], axis=1)
    chunk_decay = jnp.exp(segsum(jnp.pad(a_cum[..., -1], ((0, 0), (0, 0), (0, 0), (1, 0)))))
    states = jnp.einsum('bgezc,bcgepn->bzgepn', chunk_decay, states)[:, :-1]
    y_off = jnp.einsum('bclgn,bcgepn,bgecl->bclgep', cc, states, jnp.exp(a_cum))
    y = (y_diag + y_off + xh * d_skip.astype(jnp.float32).reshape(G, E)[:, :, None]).reshape(B, S, SSM_WIDTH)
    y = y * jax.nn.silu(z.astype(jnp.float32))
    yg = y.reshape(B, S, G, SSM_WIDTH // G)
    yg = yg * lax.rsqrt(jnp.mean(yg * yg, axis=-1, keepdims=True) + NORM_EPS)
    return yg.reshape(B, S, SSM_WIDTH) * norm_w.astype(jnp.float32)


def hybrid_layer(x, positions, norm_w, w_in, conv_w, conv_b, dt_bias, a_log, d_skip,
                 hgrn_norm_w, ssm_norm_w, w_out, lb):
    h = rms_norm(x, norm_w)
    proj = h @ w_in
    (aq, ak, av, ag, hq, hf, hi, hg, sz, sxbc, sdt) = jnp.split(
        proj, list(np.cumsum(IN_SPLITS)[:-1]), axis=-1)
    ya = attention_mixer(aq, ak, av, ag, positions)
    yh = hgrn2_mixer(hq, hf, hi, hg, lb, hgrn_norm_w)
    ys = ssd_mixer(sz, sxbc, sdt, conv_w, conv_b, dt_bias, a_log, d_skip, ssm_norm_w)
    y = jnp.concatenate([ya, yh, ys], axis=-1).astype(x.dtype) @ w_out
    return x + y


def setup_inputs(seed: int = 0) -> dict:
    key = jax.random.key(seed)
    ks = jax.random.split(key, 16)
    f32 = jnp.float32
    x = jax.random.normal(ks[0], (BATCH, SEQ, D_MODEL), f32)
    offs = jax.random.randint(ks[1], (BATCH, 1), 0, 4096, dtype=jnp.int32)
    positions = jnp.arange(SEQ, dtype=jnp.int32)[None, :] + offs
    norm_w = 1.0 + 0.02 * jax.random.normal(ks[2], (DEPTH, D_MODEL), f32)
    w_in = jax.random.normal(ks[3], (DEPTH, D_MODEL, IN_WIDTH), f32) * D_MODEL ** -0.5
    conv_w = jax.random.normal(ks[4], (DEPTH, SSM_CONV, SSM_CONV_CH), f32) * SSM_CONV ** -0.5
    conv_b = 0.02 * jax.random.normal(ks[5], (DEPTH, SSM_CONV_CH), f32)
    dt0 = jnp.exp(jax.random.uniform(ks[6], (DEPTH, SSM_HEADS), f32, jnp.log(1e-3), jnp.log(1e-1)))
    dt_bias = dt0 + jnp.log(-jnp.expm1(-dt0))
    a_log = jnp.log(jax.random.uniform(ks[7], (DEPTH, SSM_HEADS), f32, 1.0, 16.0))
    d_skip = 1.0 + 0.1 * jax.random.normal(ks[8], (DEPTH, SSM_HEADS), f32)
    hgrn_norm_w = 1.0 + 0.02 * jax.random.normal(ks[9], (DEPTH, HGRN_WIDTH), f32)
    ssm_norm_w = 1.0 + 0.02 * jax.random.normal(ks[10], (DEPTH, SSM_WIDTH), f32)
    w_out = jax.random.normal(ks[11], (DEPTH, MIX_WIDTH, D_MODEL), f32) * MIX_WIDTH ** -0.5
    hgrn_lb_logits = jax.random.normal(ks[12], (DEPTH, HGRN_KEY_WIDTH), f32)
    final_norm_w = 1.0 + 0.02 * jax.random.normal(ks[13], (D_MODEL,), f32)
    return {"x": x, "positions": positions, "norm_w": norm_w, "w_in": w_in,
            "conv_w": conv_w, "conv_b": conv_b, "dt_bias": dt_bias, "a_log": a_log,
            "d_skip": d_skip, "hgrn_norm_w": hgrn_norm_w, "ssm_norm_w": ssm_norm_w,
            "w_out": w_out, "hgrn_lb_logits": hgrn_lb_logits, "final_norm_w": final_norm_w}


def reference(x, positions, norm_w, w_in, conv_w, conv_b, dt_bias, a_log, d_skip,
              hgrn_norm_w, ssm_norm_w, w_out, hgrn_lb_logits, final_norm_w):
    p = jax.nn.softmax(hgrn_lb_logits.astype(jnp.float32), axis=0)
    cs = jnp.cumsum(p, axis=0)
    lb_all = cs - cs[0:1]
    for l in range(DEPTH):
        x = hybrid_layer(x, positions, norm_w[l], w_in[l], conv_w[l], conv_b[l], dt_bias[l],
                         a_log[l], d_skip[l], hgrn_norm_w[l], ssm_norm_w[l], w_out[l], lb_all[l])
    return rms_norm(x, final_norm_w)
```

```python
import functools

import numpy as np
import jax
import jax.numpy as jnp
from jax import lax
from jax.experimental import pallas as pl
from jax.experimental.pallas import tpu as pltpu

F32 = jnp.float32
BF16 = jnp.bfloat16

D_MODEL = 4096
DEPTH = 2
NORM_EPS = 1e-6
ATTN_HEADS = 8
ATTN_HEAD_DIM = 128
ATTN_WIDTH = ATTN_HEADS * ATTN_HEAD_DIM
ATTN_PATTERNS = ((128, 1), (512, 4), (2048, 16))
ATTN_REACH = max(w for w, _ in ATTN_PATTERNS)
ROPE_THETA = 10000.0
HGRN_HEADS = 8
HGRN_DIM = 128
HGRN_WIDTH = HGRN_HEADS * HGRN_DIM
SSM_HEADS = 32
SSM_HEAD_DIM = 64
SSM_WIDTH = SSM_HEADS * SSM_HEAD_DIM
SSM_GROUPS = 8
SSM_HEADS_PER_GROUP = SSM_HEADS // SSM_GROUPS
SSM_GROUP_WIDTH = SSM_WIDTH // SSM_GROUPS
SSM_STATE = 128
SSM_CONV = 4
MIX_WIDTH = ATTN_WIDTH + HGRN_WIDTH + SSM_WIDTH

COL_AQ, COL_AK, COL_AV, COL_AG = 0, 1024, 2048, 3072
COL_HQ, COL_HF, COL_HI, COL_HG = 4096, 5120, 6144, 7168
COL_SZ = 8192
COL_SX = 10240
COL_SB = COL_SX + SSM_WIDTH
COL_SC = COL_SB + SSM_GROUPS * SSM_STATE
COL_DT = COL_SC + SSM_GROUPS * SSM_STATE
IN_MAIN = COL_DT

LANES = 128
SUBLANES = 8
VMEM_LIMIT = 56 * 1024 * 1024

NEG = -1e30

NORM_ROWS = 512
PROJ_TM, PROJ_TN = 1024, 1024
OUT_TM, OUT_TN = 512, 1024
ROPE_ROWS = 512
ATTN_TQ = 256
HGRN_ROWS = 512
HGRN_CHUNK = 64
SSD_ROWS = 512
SSD_CHUNK = 128


def _params(*sem):
    return pltpu.CompilerParams(dimension_semantics=sem, vmem_limit_bytes=VMEM_LIMIT)


def _rmsnorm_kernel(x_ref, w_ref, o_ref):
    x = x_ref[...]
    ms = jnp.mean(x * x, axis=-1, keepdims=True)
    o_ref[...] = ((x * lax.rsqrt(ms + NORM_EPS)) * w_ref[...]).astype(o_ref.dtype)


def _rmsnorm(x2, w, out_dtype):
    m, d = x2.shape
    return pl.pallas_call(
        _rmsnorm_kernel,
        grid=(m // NORM_ROWS,),
        in_specs=[pl.BlockSpec((NORM_ROWS, d), lambda i: (i, 0)),
                  pl.BlockSpec((1, d), lambda i: (0, 0))],
        out_specs=pl.BlockSpec((NORM_ROWS, d), lambda i: (i, 0)),
        out_shape=jax.ShapeDtypeStruct((m, d), out_dtype),
        compiler_params=_params("parallel"),
        name="rmsnorm",
    )(x2, w.reshape(1, d))


def _inproj_kernel(h_ref, w_ref, wdt_ref, o_ref, odt_ref):
    h = h_ref[...]
    o_ref[...] = jnp.dot(h, w_ref[...], preferred_element_type=F32)

    @pl.when(pl.program_id(1) == 0)
    def _():
        odt_ref[...] = jnp.dot(h, wdt_ref[...], preferred_element_type=F32)


def _inproj(h, w_main, w_dt):
    m, d = h.shape
    n = w_main.shape[1]
    return pl.pallas_call(
        _inproj_kernel,
        grid=(m // PROJ_TM, n // PROJ_TN),
        in_specs=[pl.BlockSpec((PROJ_TM, d), lambda i, j: (i, 0)),
                  pl.BlockSpec((d, PROJ_TN), lambda i, j: (0, j)),
                  pl.BlockSpec((d, LANES), lambda i, j: (0, 0))],
        out_specs=[pl.BlockSpec((PROJ_TM, PROJ_TN), lambda i, j: (i, j)),
                   pl.BlockSpec((PROJ_TM, LANES), lambda i, j: (i, 0))],
        out_shape=[jax.ShapeDtypeStruct((m, n), F32),
                   jax.ShapeDtypeStruct((m, LANES), F32)],
        compiler_params=_params("parallel", "arbitrary"),
        name="inproj",
    )(h, w_main, w_dt)


def _outproj_kernel(ya_ref, yh_ref, ys_ref, wa_ref, wh_ref, ws_ref, x_ref, o_ref):
    acc = jnp.dot(ya_ref[...], wa_ref[...], preferred_element_type=F32)
    acc += jnp.dot(yh_ref[...], wh_ref[...], preferred_element_type=F32)
    acc += jnp.dot(ys_ref[...], ws_ref[...], preferred_element_type=F32)
    o_ref[...] = x_ref[...] + acc


def _outproj(ya, yh, ys, w_out, x2):
    m, d = x2.shape
    row = lambda i, j: (i, 0)
    col = lambda i, j: (0, j)
    tile = lambda i, j: (i, j)
    wa = w_out[:ATTN_WIDTH]
    wh = w_out[ATTN_WIDTH:ATTN_WIDTH + HGRN_WIDTH]
    ws = w_out[ATTN_WIDTH + HGRN_WIDTH:]
    return pl.pallas_call(
        _outproj_kernel,
        grid=(m // OUT_TM, d // OUT_TN),
        in_specs=[pl.BlockSpec((OUT_TM, ATTN_WIDTH), row),
                  pl.BlockSpec((OUT_TM, HGRN_WIDTH), row),
                  pl.BlockSpec((OUT_TM, SSM_WIDTH), row),
                  pl.BlockSpec((ATTN_WIDTH, OUT_TN), col),
                  pl.BlockSpec((HGRN_WIDTH, OUT_TN), col),
                  pl.BlockSpec((SSM_WIDTH, OUT_TN), col),
                  pl.BlockSpec((OUT_TM, OUT_TN), tile)],
        out_specs=pl.BlockSpec((OUT_TM, OUT_TN), tile),
        out_shape=jax.ShapeDtypeStruct((m, d), F32),
        compiler_params=_params("parallel", "arbitrary"),
        name="outproj",
    )(ya, yh, ys, wa, wh, ws, x2)


def _rope_table_kernel(pos_ref, invf_ref, cos_ref, sin_ref):
    ang = pos_ref[0] * invf_ref[...]
    lane = lax.broadcasted_iota(jnp.int32, ang.shape, 1)
    cos_ref[0] = jnp.cos(ang)
    s = jnp.sin(ang)
    sin_ref[0] = jnp.where(lane < ATTN_HEAD_DIM // 2, -s, s)


def _rope_tables(positions):
    b, s = positions.shape
    half = ATTN_HEAD_DIM // 2
    inv_freq = ROPE_THETA ** (-jnp.arange(half, dtype=F32) / half)
    invf = jnp.concatenate([inv_freq, inv_freq]).reshape(1, ATTN_HEAD_DIM)
    posb = jnp.broadcast_to(positions.astype(F32)[..., None], (b, s, ATTN_HEAD_DIM))
    spec = pl.BlockSpec((1, ROPE_ROWS, ATTN_HEAD_DIM), lambda bi, i: (bi, i, 0))
    return pl.pallas_call(
        _rope_table_kernel,
        grid=(b, s // ROPE_ROWS),
        in_specs=[spec, pl.BlockSpec((1, ATTN_HEAD_DIM), lambda bi, i: (0, 0))],
        out_specs=[spec, spec],
        out_shape=[jax.ShapeDtypeStruct((b, s, ATTN_HEAD_DIM), F32)] * 2,
        compiler_params=_params("parallel", "parallel"),
        name="rope_tables",
    )(posb, invf)


def _rope_kernel(pad_blocks, q_ref, k_ref, v_ref, cos_ref, sin_ref, qo_ref, ko_ref, vo_ref):
    i = pl.program_id(1)
    cos = cos_ref[0]
    sin = sin_ref[0]
    half = ATTN_HEAD_DIM // 2
    scale = ATTN_HEAD_DIM ** -0.5

    def rope(t):
        return t * cos + pltpu.roll(t, half, axis=1) * sin

    for h in range(ATTN_HEADS):
        sl = slice(h * ATTN_HEAD_DIM, (h + 1) * ATTN_HEAD_DIM)
        qo_ref[0, :, sl] = (rope(q_ref[0, :, sl]) * scale).astype(BF16)

    @pl.when(i < pad_blocks)
    def _():
        ko_ref[...] = jnp.zeros_like(ko_ref)
        vo_ref[...] = jnp.zeros_like(vo_ref)

    @pl.when(i >= pad_blocks)
    def _():
        for h in range(ATTN_HEADS):
            sl = slice(h * ATTN_HEAD_DIM, (h + 1) * ATTN_HEAD_DIM)
            ko_ref[0, :, sl] = rope(k_ref[0, :, sl]).astype(BF16)
        vo_ref[0] = v_ref[0].astype(BF16)


def _rope_apply(proj3, cos, sin):
    b, s, _ = proj3.shape
    pad_blocks = ATTN_REACH // ROPE_ROWS
    nblk = s // ROPE_ROWS + pad_blocks
    src = lambda c: (lambda bi, i: (bi, jnp.maximum(i - pad_blocks, 0), c))
    slab = lambda c: pl.BlockSpec((1, ROPE_ROWS, ATTN_WIDTH), src(c))
    tab = pl.BlockSpec((1, ROPE_ROWS, ATTN_HEAD_DIM), src(0))
    return pl.pallas_call(
        functools.partial(_rope_kernel, pad_blocks),
        grid=(b, nblk),
        in_specs=[slab(COL_AQ // ATTN_WIDTH), slab(COL_AK // ATTN_WIDTH), slab(COL_AV // ATTN_WIDTH),
                  tab, tab],
        out_specs=[pl.BlockSpec((1, ROPE_ROWS, ATTN_WIDTH), src(0)),
                   pl.BlockSpec((1, ROPE_ROWS, ATTN_WIDTH), lambda bi, i: (bi, i, 0)),
                   pl.BlockSpec((1, ROPE_ROWS, ATTN_WIDTH), lambda bi, i: (bi, i, 0))],
        out_shape=[jax.ShapeDtypeStruct((b, s, ATTN_WIDTH), BF16),
                   jax.ShapeDtypeStruct((b, s + ATTN_REACH, ATTN_WIDTH), BF16),
                   jax.ShapeDtypeStruct((b, s + ATTN_REACH, ATTN_WIDTH), BF16)],
        compiler_params=_params("parallel", "arbitrary"),
        name="rope_apply",
    )(proj3, proj3, proj3, cos, sin)


def _attn_bias_table(tq):
    r = np.arange(tq)[:, None]
    j = np.arange(ATTN_REACH + tq)[None, :]
    delta = r + ATTN_REACH - j
    count = np.zeros(delta.shape, np.int64)
    for window, dilation in ATTN_PATTERNS:
        count += (delta >= 0) & (delta % dilation == 0) & (delta <= window)
    return np.where(count > 0, np.log(np.maximum(count, 1)), NEG).astype(np.float32)


def _attn_kernel(q_ref, k_ref, v_ref, g_ref, bias_ref, o_ref):
    i = pl.program_id(2)
    tq = q_ref.shape[1]
    span = ATTN_REACH + tq
    start = pl.multiple_of(i * tq, tq)
    q = q_ref[0]
    k = k_ref[0, pl.ds(start, span), :]
    v = v_ref[0, pl.ds(start, span), :]
    s = lax.dot_general(q, k, (((1,), (1,)), ((), ())), preferred_element_type=F32)
    col = lax.broadcasted_iota(jnp.int32, (1, span), 1)
    in_seq = col >= ATTN_REACH - i * tq
    s = jnp.where(in_seq, s + bias_ref[...], NEG)
    m = jnp.max(s, axis=-1, keepdims=True)
    p = jnp.exp(s - m)
    den = jnp.sum(p, axis=-1, keepdims=True)
    o = jnp.dot(p.astype(BF16), v, preferred_element_type=F32) / den
    g = g_ref[0]
    o_ref[0] = (o * (g * jax.nn.sigmoid(g))).astype(o_ref.dtype)


def _attention(q_rot, k_pad, v_pad, proj3):
    b, s, _ = q_rot.shape
    tq = ATTN_TQ
    bias = jnp.asarray(_attn_bias_table(tq))
    gate0 = COL_AG // ATTN_HEAD_DIM
    full = pl.BlockSpec((1, s + ATTN_REACH, ATTN_HEAD_DIM), lambda bi, h, i: (bi, 0, h))
    return pl.pallas_call(
        _attn_kernel,
        grid=(b, ATTN_HEADS, s // tq),
        in_specs=[pl.BlockSpec((1, tq, ATTN_HEAD_DIM), lambda bi, h, i: (bi, i, h)),
                  full, full,
                  pl.BlockSpec((1, tq, ATTN_HEAD_DIM), lambda bi, h, i: (bi, i, gate0 + h)),
                  pl.BlockSpec(bias.shape, lambda bi, h, i: (0, 0))],
        out_specs=pl.BlockSpec((1, tq, ATTN_HEAD_DIM), lambda bi, h, i: (bi, i, h)),
        out_shape=jax.ShapeDtypeStruct((b, s, ATTN_WIDTH), BF16),
        compiler_params=_params("parallel", "parallel", "arbitrary"),
        name="attention",
    )(q_rot, k_pad, v_pad, proj3, bias)


def _cumsum_rows(x, period):
    row = lax.broadcasted_iota(jnp.int32, x.shape, 0) % period
    k = 1
    while k < period:
        x = x + jnp.where(row >= k, pltpu.roll(x, k, axis=0), 0.0)
        k *= 2
    return x


def _hgrn_chunk(q, fr, v, lb, log_lb, log1m_lb, st):
    c = q.shape[0]
    nt = (((1,), (1,)), ((), ()))
    tn = (((0,), (0,)), ((), ()))
    log_sig = jnp.minimum(fr, 0.0) - jnp.log1p(jnp.exp(-jnp.abs(fr)))
    lower = log1m_lb + log_sig
    g = jnp.maximum(log_lb, lower) + jnp.log1p(jnp.exp(-jnp.abs(log_lb - lower)))
    kk = (1.0 - lb) * jax.nn.sigmoid(-fr)
    b = _cumsum_rows(g, c)
    b_last = b[c - 1:c, :]

    o = lax.dot_general((q * jnp.exp(b)).astype(BF16), st.astype(BF16), nt,
                        preferred_element_type=F32)

    row = lax.broadcasted_iota(jnp.int32, (c, 1), 0)
    ri = lax.broadcasted_iota(jnp.int32, (c, c), 0)
    ci = lax.broadcasted_iota(jnp.int32, (c, c), 1)
    a = jnp.zeros((c, c), F32)
    m = c // 2
    while m >= SUBLANES:
        mid = jnp.concatenate(
            [jnp.broadcast_to(b[blk * 2 * m + m - 1:blk * 2 * m + m, :], (2 * m, b.shape[1]))
             for blk in range(c // (2 * m))], axis=0)
        upper = (row % (2 * m)) >= m
        qt = jnp.where(upper, q * jnp.exp(jnp.minimum(b - mid, 0.0)), 0.0)
        kt = jnp.where(upper, 0.0, kk * jnp.exp(jnp.minimum(mid - b, 0.0)))
        part = lax.dot_general(qt.astype(BF16), kt.astype(BF16), nt, preferred_element_type=F32)
        if 2 * m < c:
            part = jnp.where((ri // (2 * m)) == (ci // (2 * m)), part, 0.0)
        a = a + part
        m //= 2
    o = o + jnp.dot(a.astype(BF16), v.astype(BF16), preferred_element_type=F32)

    sub = row % SUBLANES
    ones = jnp.ones((q.shape[1], q.shape[1]), BF16)
    for d in range(SUBLANES):
        if d == 0:
            w, vs = q * kk, v
        else:
            ks = pltpu.roll(kk, d, axis=0)
            bs = pltpu.roll(b, d, axis=0)
            vs = pltpu.roll(v, d, axis=0)
            w = jnp.where(sub >= d, q * ks * jnp.exp(jnp.minimum(b - bs, 0.0)), 0.0)
        o = o + jnp.dot(w.astype(BF16), ones, preferred_element_type=F32) * vs

    kd = kk * jnp.exp(b_last - b)
    st_new = st * jnp.exp(b_last) + lax.dot_general(v.astype(BF16), kd.astype(BF16), tn,
                                                    preferred_element_type=F32)
    return o, st_new


def _hgrn_kernel(q_ref, f_ref, i_ref, g_ref, lb_ref, llb_ref, l1m_ref, nw_ref, o_ref, st_ref):
    @pl.when(pl.program_id(2) == 0)
    def _():
        st_ref[...] = jnp.zeros_like(st_ref)

    lb, log_lb, log1m_lb, nw = lb_ref[...], llb_ref[...], l1m_ref[...], nw_ref[...]
    rows = q_ref.shape[1]

    def body(ci, carry):
        r0 = pl.multiple_of(ci * HGRN_CHUNK, HGRN_CHUNK)
        sl = pl.ds(r0, HGRN_CHUNK)
        o, st = _hgrn_chunk(q_ref[0, sl, :], f_ref[0, sl, :], i_ref[0, sl, :],
                            lb, log_lb, log1m_lb, st_ref[...])
        st_ref[...] = st
        o = o * lax.rsqrt(jnp.mean(o * o, axis=-1, keepdims=True) + NORM_EPS) * nw
        g = g_ref[0, sl, :]
        o_ref[0, sl, :] = (o * (g * jax.nn.sigmoid(g))).astype(o_ref.dtype)
        return carry

    lax.fori_loop(0, rows // HGRN_CHUNK, body, 0)


def _hgrn(proj3, lb, norm_w):
    b, s, _ = proj3.shape
    lb = lb.reshape(1, HGRN_WIDTH)
    log_lb = jnp.log(lb)
    log1m_lb = jnp.log1p(-lb)
    blk = lambda c0: pl.BlockSpec((1, HGRN_ROWS, HGRN_DIM),
                                  lambda bi, h, t: (bi, t, c0 // HGRN_DIM + h))
    par = pl.BlockSpec((1, HGRN_DIM), lambda bi, h, t: (0, h))
    return pl.pallas_call(
        _hgrn_kernel,
        grid=(b, HGRN_HEADS, s // HGRN_ROWS),
        in_specs=[blk(COL_HQ), blk(COL_HF), blk(COL_HI), blk(COL_HG), par, par, par, par],
        out_specs=pl.BlockSpec((1, HGRN_ROWS, HGRN_DIM), lambda bi, h, t: (bi, t, h)),
        out_shape=jax.ShapeDtypeStruct((b, s, HGRN_WIDTH), BF16),
        scratch_shapes=[pltpu.VMEM((HGRN_DIM, HGRN_DIM), F32)],
        compiler_params=_params("parallel", "parallel", "arbitrary"),
        name="hgrn2",
    )(proj3, proj3, proj3, proj3, lb, log_lb, log1m_lb, norm_w.reshape(1, HGRN_WIDTH))


def _softplus(x):
    return jnp.maximum(x, 0.0) + jnp.log1p(jnp.exp(-jnp.abs(x)))


def _silu(x):
    return x * jax.nn.sigmoid(x)


def _cumsum_lanes(x, period):
    lane = lax.broadcasted_iota(jnp.int32, x.shape, 1) % period
    k = 1
    while k < period:
        x = x + jnp.where(lane >= k, pltpu.roll(x, k, axis=1), 0.0)
        k *= 2
    return x


def _causal_conv(cur, tail, w, bias):
    t = cur.shape[0]
    cat = jnp.concatenate([tail, cur], axis=0)
    acc = cur * w[SSM_CONV - 1:SSM_CONV, :] + bias
    for j in range(SSM_CONV - 1):
        shifted = pltpu.roll(cat, SSM_CONV - 1 - j, axis=0)[SUBLANES:SUBLANES + t]
        acc = acc + shifted * w[j:j + 1, :]
    return acc


def _ssd_kernel(z_ref, x_ref, b_ref, c_ref, dtc_ref, dtr_ref,
                cwx_ref, cwb_ref, cwc_ref, cbx_ref, cbb_ref, cbc_ref,
                dtbc_ref, dtbr_ref, alc_ref, alr_ref, dsk_ref, nw_ref,
                o_ref, st_ref, tail_ref, xs_ref, bs_ref, cs_ref, acc_ref, acr_ref, dts_ref):
    e_heads, p_dim, lc = SSM_HEADS_PER_GROUP, SSM_HEAD_DIM, SSD_CHUNK
    rows = x_ref.shape[1]
    xw, sw = SSM_GROUP_WIDTH, SSM_STATE

    @pl.when(pl.program_id(2) == 0)
    def _():
        st_ref[...] = jnp.zeros_like(st_ref)
        tail_ref[...] = jnp.zeros_like(tail_ref)

    tail = tail_ref[...]
    xr, br, cr = x_ref[0], b_ref[0], c_ref[0]
    xs_ref[...] = _silu(_causal_conv(xr, tail[:, :xw], cwx_ref[...], cbx_ref[...]))
    bs_ref[...] = _silu(_causal_conv(br, tail[:, xw:xw + sw], cwb_ref[...], cbb_ref[...]))
    cs_ref[...] = _silu(_causal_conv(cr, tail[:, xw + sw:], cwc_ref[...], cbc_ref[...]))
    tail_ref[:, :xw] = xr[rows - SUBLANES:]
    tail_ref[:, xw:xw + sw] = br[rows - SUBLANES:]
    tail_ref[:, xw + sw:] = cr[rows - SUBLANES:]

    dt_c = _softplus(dtc_ref[0, 0] + dtbc_ref[0])
    dt_r = _softplus(dtr_ref[0, 0] + dtbr_ref[0])
    dts_ref[...] = dt_c
    acc_ref[...] = _cumsum_rows(dt_c * (-jnp.exp(alc_ref[0])), lc)
    ac_rows = _cumsum_lanes(dt_r * (-jnp.exp(alr_ref[0])), lc)
    for ci in range(rows // lc):
        acr_ref[ci] = ac_rows[:, ci * lc:(ci + 1) * lc]

    li = lax.broadcasted_iota(jnp.int32, (lc, lc), 0)
    si = lax.broadcasted_iota(jnp.int32, (lc, lc), 1)
    causal = li >= si
    lane = lax.broadcasted_iota(jnp.int32, (1, xw), 1) // p_dim
    nt = (((1,), (1,)), ((), ()))
    tn = (((0,), (0,)), ((), ()))

    def expand(cols):
        out = jnp.broadcast_to(cols[:, 0:1], (cols.shape[0], xw))
        for e in range(1, e_heads):
            out = jnp.where(lane == e, jnp.broadcast_to(cols[:, e:e + 1], out.shape), out)
        return out

    def body(ci, carry):
        r0 = pl.multiple_of(ci * lc, lc)
        sl = pl.ds(r0, lc)
        xs, bm, cm = xs_ref[sl, :], bs_ref[sl, :], cs_ref[sl, :]
        ac_c = acc_ref[sl, :]
        ac_r = acr_ref[ci]
        xdt = xs * expand(dts_ref[sl, :])
        cb = lax.dot_general(cm.astype(BF16), bm.astype(BF16), nt, preferred_element_type=F32)
        y_parts = []
        for e in range(e_heads):
            dif = ac_c[:, e:e + 1] - ac_r[e:e + 1, :]
            lm = jnp.exp(jnp.where(causal, dif, NEG))
            y_parts.append(jnp.dot((cb * lm).astype(BF16),
                                   xdt[:, e * p_dim:(e + 1) * p_dim].astype(BF16),
                                   preferred_element_type=F32))
        y = jnp.concatenate(y_parts, axis=1)
        st = st_ref[...]
        y = y + jnp.dot(cm.astype(BF16), st.astype(BF16), preferred_element_type=F32) \
            * expand(jnp.exp(ac_c))
        ac_last = ac_c[lc - 1:lc, :]
        xdec = xdt * expand(jnp.exp(ac_last - ac_c))
        st_ref[...] = st * expand(jnp.exp(ac_last)) + lax.dot_general(
            bm.astype(BF16), xdec.astype(BF16), tn, preferred_element_type=F32)
        y = y + xs * dsk_ref[...]
        y = y * _silu(z_ref[0, sl, :])
        y = y * lax.rsqrt(jnp.mean(y * y, axis=-1, keepdims=True) + NORM_EPS) * nw_ref[...]
        o_ref[0, sl, :] = y.astype(o_ref.dtype)
        return carry

    lax.fori_loop(0, rows // lc, body, 0)


def _ssd(proj3, dt3, conv_w, conv_b, dt_bias, a_log, d_skip, norm_w):
    b, s, _ = proj3.shape
    g, e = SSM_GROUPS, SSM_HEADS_PER_GROUP
    rows = SSD_ROWS
    dt_cols = dt3[:, :, :SSM_HEADS].reshape(b, s, g, e).transpose(0, 2, 1, 3)
    dt_rows = dt_cols.transpose(0, 1, 3, 2)
    cw_x, cw_b, cw_c = (conv_w[:, :SSM_WIDTH], conv_w[:, SSM_WIDTH:SSM_WIDTH + g * SSM_STATE],
                        conv_w[:, SSM_WIDTH + g * SSM_STATE:])
    cb = conv_b.reshape(1, -1)
    cb_x, cb_b, cb_c = (cb[:, :SSM_WIDTH], cb[:, SSM_WIDTH:SSM_WIDTH + g * SSM_STATE],
                        cb[:, SSM_WIDTH + g * SSM_STATE:])
    dsk = jnp.repeat(d_skip, SSM_HEAD_DIM).reshape(1, SSM_WIDTH)

    def act(c0, width):
        return pl.BlockSpec((1, rows, width), lambda bi, gi, t: (bi, t, c0 // width + gi))

    def par(nrows, width):
        return pl.BlockSpec((nrows, width), lambda bi, gi, t: (0, gi))

    head_c = pl.BlockSpec((1, 1, e), lambda bi, gi, t: (gi, 0, 0))
    head_r = pl.BlockSpec((1, e, 1), lambda bi, gi, t: (gi, 0, 0))
    return pl.pallas_call(
        _ssd_kernel,
        grid=(b, g, s // rows),
        in_specs=[act(COL_SZ, SSM_GROUP_WIDTH), act(COL_SX, SSM_GROUP_WIDTH),
                  act(COL_SB, SSM_STATE), act(COL_SC, SSM_STATE),
                  pl.BlockSpec((1, 1, rows, e), lambda bi, gi, t: (bi, gi, t, 0)),
                  pl.BlockSpec((1, 1, e, rows), lambda bi, gi, t: (bi, gi, 0, t)),
                  par(SSM_CONV, SSM_GROUP_WIDTH), par(SSM_CONV, SSM_STATE), par(SSM_CONV, SSM_STATE),
                  par(1, SSM_GROUP_WIDTH), par(1, SSM_STATE), par(1, SSM_STATE),
                  head_c, head_r, head_c, head_r,
                  par(1, SSM_GROUP_WIDTH), par(1, SSM_GROUP_WIDTH)],
        out_specs=pl.BlockSpec((1, rows, SSM_GROUP_WIDTH), lambda bi, gi, t: (bi, t, gi)),
        out_shape=jax.ShapeDtypeStruct((b, s, SSM_WIDTH), BF16),
        scratch_shapes=[pltpu.VMEM((SSM_STATE, SSM_GROUP_WIDTH), F32),
                        pltpu.VMEM((SUBLANES, SSM_GROUP_WIDTH + 2 * SSM_STATE), F32),
                        pltpu.VMEM((rows, SSM_GROUP_WIDTH), F32),
                        pltpu.VMEM((rows, SSM_STATE), F32),
                        pltpu.VMEM((rows, SSM_STATE), F32),
                        pltpu.VMEM((rows, e), F32),
                        pltpu.VMEM((rows // SSD_CHUNK, e, SSD_CHUNK), F32),
                        pltpu.VMEM((rows, e), F32)],
        compiler_params=_params("parallel", "parallel", "arbitrary"),
        name="ssd",
    )(proj3, proj3, proj3, proj3, dt_cols, dt_rows,
      cw_x, cw_b, cw_c, cb_x, cb_b, cb_c,
      dt_bias.reshape(g, 1, e), dt_bias.reshape(g, e, 1),
      a_log.reshape(g, 1, e), a_log.reshape(g, e, 1),
      dsk, norm_w.reshape(1, SSM_WIDTH))


def _layer(x2, batch, cos, sin, norm_w, w_in, conv_w, conv_b, dt_bias, a_log, d_skip,
           hgrn_norm_w, ssm_norm_w, w_out, lb):
    m, d = x2.shape
    s = m // batch
    h = _rmsnorm(x2, norm_w, BF16)
    w_main = w_in[:, :IN_MAIN].astype(BF16)
    w_dt = jnp.pad(w_in[:, IN_MAIN:], ((0, 0), (0, LANES - SSM_HEADS))).astype(BF16)
    proj, dt = _inproj(h, w_main, w_dt)
    proj3 = proj.reshape(batch, s, IN_MAIN)
    dt3 = dt.reshape(batch, s, LANES)
    q_rot, k_pad, v_pad = _rope_apply(proj3, cos, sin)
    ya = _attention(q_rot, k_pad, v_pad, proj3)
    yh = _hgrn(proj3, lb, hgrn_norm_w)
    ys = _ssd(proj3, dt3, conv_w, conv_b, dt_bias, a_log, d_skip, ssm_norm_w)
    return _outproj(ya.reshape(m, -1), yh.reshape(m, -1), ys.reshape(m, -1),
                    w_out.astype(BF16), x2)


def kernel(x, positions, norm_w, w_in, conv_w, conv_b, dt_bias, a_log, d_skip, hgrn_norm_w,
           ssm_norm_w, w_out, hgrn_lb_logits, final_norm_w):
    batch, s, d = x.shape
    p = jax.nn.softmax(hgrn_lb_logits.astype(F32), axis=0)
    cs = jnp.cumsum(p, axis=0)
    lb_all = cs - cs[0:1]
    cos, sin = _rope_tables(positions)
    x2 = x.reshape(batch * s, d)
    for l in range(DEPTH):
        x2 = _layer(x2, batch, cos, sin, norm_w[l], w_in[l], conv_w[l], conv_b[l], dt_bias[l],
                    a_log[l], d_skip[l], hgrn_norm_w[l], ssm_norm_w[l], w_out[l], lb_all[l])
    return _rmsnorm(x2, final_norm_w, F32).reshape(batch, s, d)
```

```python
import functools

import numpy as np
import jax
import jax.numpy as jnp
from jax import lax
from jax.experimental import pallas as pl
from jax.experimental.pallas import tpu as pltpu

F32 = jnp.float32
BF16 = jnp.bfloat16

D_MODEL = 4096
DEPTH = 2
NORM_EPS = 1e-6
ATTN_HEADS = 8
ATTN_HEAD_DIM = 128
ATTN_WIDTH = ATTN_HEADS * ATTN_HEAD_DIM
ATTN_PATTERNS = ((128, 1), (512, 4), (2048, 16))
ATTN_REACH = max(w for w, _ in ATTN_PATTERNS)
ROPE_THETA = 10000.0
HGRN_HEADS = 8
HGRN_DIM = 128
HGRN_WIDTH = HGRN_HEADS * HGRN_DIM
SSM_HEADS = 32
SSM_HEAD_DIM = 64
SSM_WIDTH = SSM_HEADS * SSM_HEAD_DIM
SSM_GROUPS = 8
SSM_HEADS_PER_GROUP = SSM_HEADS // SSM_GROUPS
SSM_GROUP_WIDTH = SSM_WIDTH // SSM_GROUPS
SSM_STATE = 128
SSM_CONV = 4
MIX_WIDTH = ATTN_WIDTH + HGRN_WIDTH + SSM_WIDTH

COL_AQ, COL_AK, COL_AV, COL_AG = 0, 1024, 2048, 3072
COL_HQ, COL_HF, COL_HI, COL_HG = 4096, 5120, 6144, 7168
COL_SZ = 8192
COL_SX = 10240
COL_SB = COL_SX + SSM_WIDTH
COL_SC = COL_SB + SSM_GROUPS * SSM_STATE
COL_DT = COL_SC + SSM_GROUPS * SSM_STATE
IN_MAIN = COL_DT

LANES = 128
SUBLANES = 8
VMEM_LIMIT = 56 * 1024 * 1024

NEG = -1e30

NORM_ROWS = 512
PROJ_TM, PROJ_TN = 1024, 1024
OUT_TM, OUT_TN = 512, 1024
ROPE_ROWS = 512
ATTN_TQ = 256
HGRN_ROWS = 512
HGRN_CHUNK = 128
HGRN_HEADS_PER_STEP = 4
SSD_ROWS = 512
SSD_CHUNK = 128
SSD_DT_ROWS = 1024


def _params(*sem):
    return pltpu.CompilerParams(dimension_semantics=sem, vmem_limit_bytes=VMEM_LIMIT)


def _rmsnorm_kernel(x_ref, w_ref, o_ref):
    x = x_ref[...]
    ms = jnp.mean(x * x, axis=-1, keepdims=True)
    o_ref[...] = ((x * lax.rsqrt(ms + NORM_EPS)) * w_ref[...]).astype(o_ref.dtype)


def _rmsnorm(x2, w, out_dtype):
    m, d = x2.shape
    return pl.pallas_call(
        _rmsnorm_kernel,
        grid=(m // NORM_ROWS,),
        in_specs=[pl.BlockSpec((NORM_ROWS, d), lambda i: (i, 0)),
                  pl.BlockSpec((1, d), lambda i: (0, 0))],
        out_specs=pl.BlockSpec((NORM_ROWS, d), lambda i: (i, 0)),
        out_shape=jax.ShapeDtypeStruct((m, d), out_dtype),
        compiler_params=_params("parallel"),
        name="rmsnorm",
    )(x2, w.reshape(1, d))


def _inproj_kernel(h_ref, w_ref, wdt_ref, o_ref, odt_ref):
    h = h_ref[...]
    o_ref[...] = jnp.dot(h, w_ref[...], preferred_element_type=F32)

    @pl.when(pl.program_id(1) == 0)
    def _():
        odt_ref[...] = jnp.dot(h, wdt_ref[...], preferred_element_type=F32)


def _inproj(h, w_main, w_dt):
    m, d = h.shape
    n = w_main.shape[1]
    return pl.pallas_call(
        _inproj_kernel,
        grid=(m // PROJ_TM, n // PROJ_TN),
        in_specs=[pl.BlockSpec((PROJ_TM, d), lambda i, j: (i, 0)),
                  pl.BlockSpec((d, PROJ_TN), lambda i, j: (0, j)),
                  pl.BlockSpec((d, LANES), lambda i, j: (0, 0))],
        out_specs=[pl.BlockSpec((PROJ_TM, PROJ_TN), lambda i, j: (i, j)),
                   pl.BlockSpec((PROJ_TM, LANES), lambda i, j: (i, 0))],
        out_shape=[jax.ShapeDtypeStruct((m, n), F32),
                   jax.ShapeDtypeStruct((m, LANES), F32)],
        compiler_params=_params("parallel", "arbitrary"),
        name="inproj",
    )(h, w_main, w_dt)


def _outproj_kernel(ya_ref, yh_ref, ys_ref, wa_ref, wh_ref, ws_ref, x_ref, o_ref):
    acc = jnp.dot(ya_ref[...], wa_ref[...], preferred_element_type=F32)
    acc += jnp.dot(yh_ref[...], wh_ref[...], preferred_element_type=F32)
    acc += jnp.dot(ys_ref[...], ws_ref[...], preferred_element_type=F32)
    o_ref[...] = x_ref[...] + acc


def _outproj(ya, yh, ys, w_out, x2):
    m, d = x2.shape
    row = lambda i, j: (i, 0)
    col = lambda i, j: (0, j)
    tile = lambda i, j: (i, j)
    assert HGRN_WIDTH == ATTN_WIDTH and SSM_WIDTH == ATTN_WIDTH + HGRN_WIDTH
    return pl.pallas_call(
        _outproj_kernel,
        grid=(m // OUT_TM, d // OUT_TN),
        in_specs=[pl.BlockSpec((OUT_TM, ATTN_WIDTH), row),
                  pl.BlockSpec((OUT_TM, HGRN_WIDTH), row),
                  pl.BlockSpec((OUT_TM, SSM_WIDTH), row),
                  pl.BlockSpec((ATTN_WIDTH, OUT_TN), col),
                  pl.BlockSpec((HGRN_WIDTH, OUT_TN), lambda i, j: (1, j)),
                  pl.BlockSpec((SSM_WIDTH, OUT_TN), lambda i, j: (1, j)),
                  pl.BlockSpec((OUT_TM, OUT_TN), tile)],
        out_specs=pl.BlockSpec((OUT_TM, OUT_TN), tile),
        out_shape=jax.ShapeDtypeStruct((m, d), F32),
        compiler_params=_params("parallel", "arbitrary"),
        name="outproj",
    )(ya, yh, ys, w_out, w_out, w_out, x2)


def _rope_table_kernel(pos_ref, invf_ref, cos_ref, sin_ref):
    ang = pos_ref[0] * invf_ref[...]
    lane = lax.broadcasted_iota(jnp.int32, ang.shape, 1)
    cos_ref[0] = jnp.cos(ang)
    s = jnp.sin(ang)
    sin_ref[0] = jnp.where(lane < ATTN_HEAD_DIM // 2, -s, s)


def _rope_tables(positions):
    b, s = positions.shape
    half = ATTN_HEAD_DIM // 2
    inv_freq = ROPE_THETA ** (-jnp.arange(half, dtype=F32) / half)
    invf = jnp.concatenate([inv_freq, inv_freq]).reshape(1, ATTN_HEAD_DIM)
    posb = jnp.broadcast_to(positions.astype(F32)[..., None], (b, s, ATTN_HEAD_DIM))
    spec = pl.BlockSpec((1, ROPE_ROWS, ATTN_HEAD_DIM), lambda bi, i: (bi, i, 0))
    return pl.pallas_call(
        _rope_table_kernel,
        grid=(b, s // ROPE_ROWS),
        in_specs=[spec, pl.BlockSpec((1, ATTN_HEAD_DIM), lambda bi, i: (0, 0))],
        out_specs=[spec, spec],
        out_shape=[jax.ShapeDtypeStruct((b, s, ATTN_HEAD_DIM), F32)] * 2,
        compiler_params=_params("parallel", "parallel"),
        name="rope_tables",
    )(posb, invf)


def _rope_kernel(pad_blocks, q_ref, k_ref, v_ref, cos_ref, sin_ref, qo_ref, ko_ref, vo_ref):
    i = pl.program_id(1)
    cos = cos_ref[0]
    sin = sin_ref[0]
    half = ATTN_HEAD_DIM // 2
    scale = ATTN_HEAD_DIM ** -0.5

    def rope(t):
        return t * cos + pltpu.roll(t, half, axis=1) * sin

    for h in range(ATTN_HEADS):
        sl = slice(h * ATTN_HEAD_DIM, (h + 1) * ATTN_HEAD_DIM)
        qo_ref[0, :, sl] = (rope(q_ref[0, :, sl]) * scale).astype(BF16)

    @pl.when(i < pad_blocks)
    def _():
        ko_ref[...] = jnp.zeros_like(ko_ref)
        vo_ref[...] = jnp.zeros_like(vo_ref)

    @pl.when(i >= pad_blocks)
    def _():
        for h in range(ATTN_HEADS):
            sl = slice(h * ATTN_HEAD_DIM, (h + 1) * ATTN_HEAD_DIM)
            ko_ref[0, :, sl] = rope(k_ref[0, :, sl]).astype(BF16)
        vo_ref[0] = v_ref[0].astype(BF16)


def _rope_apply(proj3, cos, sin):
    b, s, _ = proj3.shape
    pad_blocks = ATTN_REACH // ROPE_ROWS
    nblk = s // ROPE_ROWS + pad_blocks
    src = lambda c: (lambda bi, i: (bi, jnp.maximum(i - pad_blocks, 0), c))
    slab = lambda c: pl.BlockSpec((1, ROPE_ROWS, ATTN_WIDTH), src(c))
    tab = pl.BlockSpec((1, ROPE_ROWS, ATTN_HEAD_DIM), src(0))
    return pl.pallas_call(
        functools.partial(_rope_kernel, pad_blocks),
        grid=(b, nblk),
        in_specs=[slab(COL_AQ // ATTN_WIDTH), slab(COL_AK // ATTN_WIDTH), slab(COL_AV // ATTN_WIDTH),
                  tab, tab],
        out_specs=[pl.BlockSpec((1, ROPE_ROWS, ATTN_WIDTH), src(0)),
                   pl.BlockSpec((1, ROPE_ROWS, ATTN_WIDTH), lambda bi, i: (bi, i, 0)),
                   pl.BlockSpec((1, ROPE_ROWS, ATTN_WIDTH), lambda bi, i: (bi, i, 0))],
        out_shape=[jax.ShapeDtypeStruct((b, s, ATTN_WIDTH), BF16),
                   jax.ShapeDtypeStruct((b, s + ATTN_REACH, ATTN_WIDTH), BF16),
                   jax.ShapeDtypeStruct((b, s + ATTN_REACH, ATTN_WIDTH), BF16)],
        compiler_params=_params("parallel", "arbitrary"),
        name="rope_apply",
    )(proj3, proj3, proj3, cos, sin)


def _attn_bias_table(tq):
    r = np.arange(tq)[:, None]
    j = np.arange(ATTN_REACH + tq)[None, :]
    delta = r + ATTN_REACH - j
    count = np.zeros(delta.shape, np.int64)
    for window, dilation in ATTN_PATTERNS:
        count += (delta >= 0) & (delta % dilation == 0) & (delta <= window)
    return np.where(count > 0, np.log(np.maximum(count, 1)), NEG).astype(np.float32)


def _attn_kernel(q_ref, k_ref, v_ref, g_ref, bias_ref, o_ref):
    i = pl.program_id(2)
    tq = q_ref.shape[1]
    span = ATTN_REACH + tq
    start = pl.multiple_of(i * tq, tq)
    q = q_ref[0]
    k = k_ref[0, pl.ds(start, span), :]
    v = v_ref[0, pl.ds(start, span), :]
    s = lax.dot_general(q, k, (((1,), (1,)), ((), ())), preferred_element_type=F32)
    col = lax.broadcasted_iota(jnp.int32, (1, span), 1)
    in_seq = col >= ATTN_REACH - i * tq
    s = jnp.where(in_seq, s + bias_ref[...], NEG)
    m = jnp.max(s, axis=-1, keepdims=True)
    p = jnp.exp(s - m)
    den = jnp.sum(p, axis=-1, keepdims=True)
    o = jnp.dot(p.astype(BF16), v, preferred_element_type=F32) / den
    g = g_ref[0]
    o_ref[0] = (o * (g * jax.nn.sigmoid(g))).astype(o_ref.dtype)


def _attention(q_rot, k_pad, v_pad, proj3):
    b, s, _ = q_rot.shape
    tq = ATTN_TQ
    bias = jnp.asarray(_attn_bias_table(tq))
    gate0 = COL_AG // ATTN_HEAD_DIM
    full = pl.BlockSpec((1, s + ATTN_REACH, ATTN_HEAD_DIM), lambda bi, h, i: (bi, 0, h))
    return pl.pallas_call(
        _attn_kernel,
        grid=(b, ATTN_HEADS, s // tq),
        in_specs=[pl.BlockSpec((1, tq, ATTN_HEAD_DIM), lambda bi, h, i: (bi, i, h)),
                  full, full,
                  pl.BlockSpec((1, tq, ATTN_HEAD_DIM), lambda bi, h, i: (bi, i, gate0 + h)),
                  pl.BlockSpec(bias.shape, lambda bi, h, i: (0, 0))],
        out_specs=pl.BlockSpec((1, tq, ATTN_HEAD_DIM), lambda bi, h, i: (bi, i, h)),
        out_shape=jax.ShapeDtypeStruct((b, s, ATTN_WIDTH), BF16),
        compiler_params=_params("parallel", "parallel", "arbitrary"),
        name="attention",
    )(q_rot, k_pad, v_pad, proj3, bias)


def _split3(x):
    hi = x.astype(BF16)
    r1 = x - hi.astype(F32)
    mid = r1.astype(BF16)
    lo = (r1 - mid.astype(F32)).astype(BF16)
    return hi, mid, lo


def _cumsum_rows_mxu(x, tril):
    n = x.shape[1]
    r = jnp.dot(tril, jnp.concatenate(_split3(x), axis=1), preferred_element_type=F32)
    return (r[:, :n] + r[:, n:2 * n]) + r[:, 2 * n:]


def _log1p_exp_neg_abs(x):
    return jnp.log(1.0 + jnp.exp(-jnp.abs(x)))


def _hgrn_pair_level(c):
    t = lax.broadcasted_iota(jnp.int32, (c, c), 0)
    s = lax.broadcasted_iota(jnp.int32, (c, c), 1)
    x = t ^ s
    level = jnp.full((c, c), -1, jnp.int32)
    m = 1
    while m < c:
        level = level + (x >= m).astype(jnp.int32)
        m *= 2
    return jnp.where(t >= s, level, -2)


def _hgrn_chunk(q, fr, v, lb, log_lb, log1m_lb, st, tril, pair_level):
    c = q.shape[0]
    nt = (((1,), (1,)), ((), ()))
    tn = (((0,), (0,)), ((), ()))
    log_sig = jnp.minimum(fr, 0.0) - _log1p_exp_neg_abs(fr)
    lower = log1m_lb + log_sig
    g = jnp.maximum(log_lb, lower) + _log1p_exp_neg_abs(log_lb - lower)
    kk = (1.0 - lb) * jax.nn.sigmoid(-fr)
    b = _cumsum_rows_mxu(g, tril)
    b_last = b[c - 1:c, :]

    o = lax.dot_general((q * jnp.exp(b)).astype(BF16), st.astype(BF16), nt,
                        preferred_element_type=F32)

    n = q.shape[1]
    a = jnp.where(pair_level == -1,
                  lax.dot_general(q.astype(BF16), kk.astype(BF16), nt, preferred_element_type=F32), 0.0)
    b3 = b.reshape(c // SUBLANES, SUBLANES, n)
    sub3 = lax.broadcasted_iota(jnp.int32, b3.shape, 1)
    level, m = 0, 1
    while m < c:
        if m >= SUBLANES:
            mid = jnp.concatenate(
                [jnp.broadcast_to(b[blk * 2 * m + m - 1:blk * 2 * m + m, :], (2 * m, n))
                 for blk in range(c // (2 * m))], axis=0)
        else:
            mid3 = jnp.broadcast_to(b3[:, m - 1:m, :], b3.shape)
            for first in range(2 * m, SUBLANES, 2 * m):
                mid3 = jnp.where(sub3 >= first,
                                 jnp.broadcast_to(b3[:, first + m - 1:first + m, :], b3.shape), mid3)
            mid = mid3.reshape(c, n)
        e = jnp.exp(-jnp.abs(b - mid))
        part = lax.dot_general((q * e).astype(BF16), (kk * e).astype(BF16), nt,
                               preferred_element_type=F32)
        a = jnp.where(pair_level == level, part, a)
        level, m = level + 1, 2 * m
    o = o + jnp.dot(a.astype(BF16), v.astype(BF16), preferred_element_type=F32)

    kd = kk * jnp.exp(b_last - b)
    st_new = st * jnp.exp(b_last) + lax.dot_general(v.astype(BF16), kd.astype(BF16), tn,
                                                    preferred_element_type=F32)
    return o, st_new


def _hgrn_kernel(q_ref, f_ref, i_ref, g_ref, lb_ref, llb_ref, l1m_ref, nw_ref, o_ref, st_ref):
    @pl.when(pl.program_id(2) == 0)
    def _():
        st_ref[...] = jnp.zeros_like(st_ref)

    rows = q_ref.shape[1]
    ri = lax.broadcasted_iota(jnp.int32, (HGRN_CHUNK, HGRN_CHUNK), 0)
    ci_ = lax.broadcasted_iota(jnp.int32, (HGRN_CHUNK, HGRN_CHUNK), 1)
    tril = (ri >= ci_).astype(BF16)
    pair_level = _hgrn_pair_level(HGRN_CHUNK)

    def body(ci, carry):
        r0 = pl.multiple_of(ci * HGRN_CHUNK, HGRN_CHUNK)
        sl = pl.ds(r0, HGRN_CHUNK)
        for hh in range(HGRN_HEADS_PER_STEP):
            hs = slice(hh * HGRN_DIM, (hh + 1) * HGRN_DIM)
            o, st = _hgrn_chunk(q_ref[0, sl, hs], f_ref[0, sl, hs], i_ref[0, sl, hs],
                                lb_ref[:, hs], llb_ref[:, hs], l1m_ref[:, hs], st_ref[hh], tril,
                                pair_level)
            st_ref[hh] = st
            o = o * lax.rsqrt(jnp.mean(o * o, axis=-1, keepdims=True) + NORM_EPS) * nw_ref[:, hs]
            g = g_ref[0, sl, hs]
            o_ref[0, sl, hs] = (o * (g * jax.nn.sigmoid(g))).astype(o_ref.dtype)
        return carry

    lax.fori_loop(0, rows // HGRN_CHUNK, body, 0)


def _hgrn(proj3, lb, norm_w):
    b, s, _ = proj3.shape
    lb = lb.reshape(1, HGRN_WIDTH)
    log_lb = jnp.log(lb)
    log1m_lb = jnp.log1p(-lb)
    width = HGRN_HEADS_PER_STEP * HGRN_DIM
    blk = lambda c0: pl.BlockSpec((1, HGRN_ROWS, width), lambda bi, h, t: (bi, t, c0 // width + h))
    par = pl.BlockSpec((1, width), lambda bi, h, t: (0, h))
    return pl.pallas_call(
        _hgrn_kernel,
        grid=(b, HGRN_HEADS // HGRN_HEADS_PER_STEP, s // HGRN_ROWS),
        in_specs=[blk(COL_HQ), blk(COL_HF), blk(COL_HI), blk(COL_HG), par, par, par, par],
        out_specs=pl.BlockSpec((1, HGRN_ROWS, width), lambda bi, h, t: (bi, t, h)),
        out_shape=jax.ShapeDtypeStruct((b, s, HGRN_WIDTH), BF16),
        scratch_shapes=[pltpu.VMEM((HGRN_HEADS_PER_STEP, HGRN_DIM, HGRN_DIM), F32)],
        compiler_params=_params("parallel", "parallel", "arbitrary"),
        name="hgrn2",
    )(proj3, proj3, proj3, proj3, lb, log_lb, log1m_lb, norm_w.reshape(1, HGRN_WIDTH))


def _softplus(x):
    return jnp.maximum(x, 0.0) + _log1p_exp_neg_abs(x)


def _silu(x):
    return x * jax.nn.sigmoid(x)


def _cumsum_lanes(x):
    n = x.shape[1]
    lane = lax.broadcasted_iota(jnp.int32, x.shape, 1)
    k = 1
    while k < n:
        x = x + jnp.where(lane >= k, pltpu.roll(x, k, axis=1), 0.0)
        k *= 2
    return x


def _ssd_dt_kernel(dt_ref, bias_ref, alog_ref, acrow_ref, accol_ref, dtcol_ref):
    heads, lc = SSM_HEADS, SSD_CHUNK
    neg_a = jnp.exp(alog_ref[...])
    zeros = jnp.zeros((LANES - 3 * heads, lc), F32)
    for c in range(dt_ref.shape[1] // lc):
        sl = slice(c * lc, (c + 1) * lc)
        raw = dt_ref[0, sl, :].T[:heads]
        dt = _softplus(raw + bias_ref[...])
        ac = _cumsum_lanes(dt * (-neg_a))
        acrow_ref[0, :, sl] = ac
        for val, dst in ((ac, accol_ref), (dt, dtcol_ref)):
            pieces = [p.astype(F32) for p in _split3(val)]
            dst[0, sl, :] = jnp.concatenate(pieces + [zeros], axis=0).T.astype(BF16)


def _ssd_dt(dt3, dt_bias, a_log):
    b, s, _ = dt3.shape
    rows = SSD_DT_ROWS
    col = pl.BlockSpec((1, rows, LANES), lambda bi, t: (bi, t, 0))
    par = pl.BlockSpec((SSM_HEADS, 1), lambda bi, t: (0, 0))
    return pl.pallas_call(
        _ssd_dt_kernel,
        grid=(b, s // rows),
        in_specs=[col, par, par],
        out_specs=[pl.BlockSpec((1, SSM_HEADS, rows), lambda bi, t: (bi, 0, t)), col, col],
        out_shape=[jax.ShapeDtypeStruct((b, SSM_HEADS, s), F32),
                   jax.ShapeDtypeStruct((b, s, LANES), BF16),
                   jax.ShapeDtypeStruct((b, s, LANES), BF16)],
        compiler_params=_params("parallel", "parallel"),
        name="ssd_dt",
    )(dt3, dt_bias.reshape(SSM_HEADS, 1), a_log.reshape(SSM_HEADS, 1))


SEL_AC_WIDE = SSM_HEADS_PER_GROUP * LANES
SEL_AC_X = SEL_AC_WIDE + SSM_GROUP_WIDTH
SEL_WIDTH = SEL_AC_X + SSM_GROUP_WIDTH


def _ssd_select_matrices():
    sel = np.zeros((SSM_GROUPS, 2 * LANES, SEL_WIDTH), np.float32)
    for g in range(SSM_GROUPS):
        for e in range(SSM_HEADS_PER_GROUP):
            head = g * SSM_HEADS_PER_GROUP + e
            for piece in range(3):
                r = piece * SSM_HEADS + head
                sel[g, r, e * LANES:(e + 1) * LANES] = 1.0
                sel[g, r, SEL_AC_WIDE + e * SSM_HEAD_DIM:SEL_AC_WIDE + (e + 1) * SSM_HEAD_DIM] = 1.0
                sel[g, LANES + r, SEL_AC_X + e * SSM_HEAD_DIM:SEL_AC_X + (e + 1) * SSM_HEAD_DIM] = 1.0
    return sel


def _ssd_kernel(z_ref, x_ref, b_ref, c_ref, acrow_ref, accol_ref, dtcol_ref, sel_ref,
                cw_ref, cb_ref, dsk_ref, nw_ref, o_ref, st_ref, stage_ref, acr_ref):
    e_heads, p_dim, lc = SSM_HEADS_PER_GROUP, SSM_HEAD_DIM, SSD_CHUNK
    rows = x_ref.shape[1]
    xw, sw = SSM_GROUP_WIDTH, SSM_STATE

    @pl.when(pl.program_id(2) == 0)
    def _():
        st_ref[...] = jnp.zeros_like(st_ref)
        stage_ref[:SUBLANES, :] = jnp.zeros((SUBLANES, stage_ref.shape[1]), F32)

    stage_ref[SUBLANES:, :xw] = x_ref[0]
    stage_ref[SUBLANES:, xw:xw + sw] = b_ref[0]
    stage_ref[SUBLANES:, xw + sw:] = c_ref[0]
    for ci in range(rows // lc):
        acr_ref[ci] = acrow_ref[0, 0, :, ci * lc:(ci + 1) * lc]

    li = lax.broadcasted_iota(jnp.int32, (lc, lc), 0)
    si = lax.broadcasted_iota(jnp.int32, (lc, lc), 1)
    causal = li >= si
    lane_head = lax.broadcasted_iota(jnp.int32, (1, xw), 1) // p_dim
    nt = (((1,), (1,)), ((), ()))
    tn = (((0,), (0,)), ((), ()))
    cw = cw_ref[0]
    cbias = cb_ref[0]

    def body(ci, carry):
        r0 = pl.multiple_of(ci * lc, lc)
        sl = pl.ds(r0, lc)
        win = stage_ref[pl.ds(r0, lc + SUBLANES), :]
        acc = win[SUBLANES:] * cw[SSM_CONV - 1:SSM_CONV, :] + cbias
        for j in range(SSM_CONV - 1):
            acc = acc + pltpu.roll(win, SSM_CONV - 1 - j, axis=0)[SUBLANES:] * cw[j:j + 1, :]
        xbc = _silu(acc)
        xs, bm, cm = xbc[:, :xw], xbc[:, xw:xw + sw].astype(BF16), xbc[:, xw + sw:].astype(BF16)

        pieces = jnp.concatenate([accol_ref[0, sl, :], dtcol_ref[0, sl, :]], axis=1)
        bc = jnp.dot(pieces, sel_ref[0], preferred_element_type=F32)
        ac_x = bc[:, SEL_AC_WIDE:SEL_AC_X]
        xdt = xs * bc[:, SEL_AC_X:]
        ac_last = ac_x[lc - 1:lc, :]
        ac_r = acr_ref[ci]

        cb = lax.dot_general(cm, bm, nt, preferred_element_type=F32)
        decayed, x_heads = [], []
        for e in range(e_heads):
            dif = bc[:, e * LANES:(e + 1) * LANES] - ac_r[e:e + 1, :]
            decayed.append((cb * jnp.exp(jnp.where(causal, dif, NEG))).astype(BF16))
            x_heads.append(jnp.where(lane_head == e, xdt, 0.0).astype(BF16))
        y = jnp.dot(jnp.concatenate(decayed, axis=1), jnp.concatenate(x_heads, axis=0),
                    preferred_element_type=F32)
        st = st_ref[...]
        y = y + jnp.dot(cm, st.astype(BF16), preferred_element_type=F32) * jnp.exp(ac_x)
        xdec = (xdt * jnp.exp(ac_last - ac_x)).astype(BF16)
        st_ref[...] = st * jnp.exp(ac_last) + lax.dot_general(bm, xdec, tn,
                                                              preferred_element_type=F32)
        y = y + xs * dsk_ref[...]
        y = y * _silu(z_ref[0, sl, :])
        y = y * lax.rsqrt(jnp.mean(y * y, axis=-1, keepdims=True) + NORM_EPS) * nw_ref[...]
        o_ref[0, sl, :] = y.astype(o_ref.dtype)
        return carry

    lax.fori_loop(0, rows // lc, body, 0)
    stage_ref[:SUBLANES, :] = stage_ref[rows:rows + SUBLANES, :]


def _ssd(proj3, dt3, conv_w, conv_b, dt_bias, a_log, d_skip, norm_w):
    b, s, _ = proj3.shape
    g, e = SSM_GROUPS, SSM_HEADS_PER_GROUP
    rows = SSD_ROWS
    ac_row, ac_col, dt_col = _ssd_dt(dt3, dt_bias, a_log)
    ac_row = ac_row.reshape(b, g, e, s)
    sel = jnp.asarray(_ssd_select_matrices(), BF16)

    def per_group(p):
        px = p[:, :SSM_WIDTH].reshape(-1, g, SSM_GROUP_WIDTH)
        pb = p[:, SSM_WIDTH:SSM_WIDTH + g * SSM_STATE].reshape(-1, g, SSM_STATE)
        pc = p[:, SSM_WIDTH + g * SSM_STATE:].reshape(-1, g, SSM_STATE)
        return jnp.concatenate([px, pb, pc], axis=2).transpose(1, 0, 2)

    conv_width = SSM_GROUP_WIDTH + 2 * SSM_STATE
    dsk = jnp.repeat(d_skip, SSM_HEAD_DIM).reshape(1, SSM_WIDTH)

    def act(c0, width):
        return pl.BlockSpec((1, rows, width), lambda bi, gi, t: (bi, t, c0 // width + gi))

    col = pl.BlockSpec((1, rows, LANES), lambda bi, gi, t: (bi, t, 0))
    grp = lambda nrows, width: pl.BlockSpec((1, nrows, width), lambda bi, gi, t: (gi, 0, 0))
    par = pl.BlockSpec((1, SSM_GROUP_WIDTH), lambda bi, gi, t: (0, gi))
    return pl.pallas_call(
        _ssd_kernel,
        grid=(b, g, s // rows),
        in_specs=[act(COL_SZ, SSM_GROUP_WIDTH), act(COL_SX, SSM_GROUP_WIDTH),
                  act(COL_SB, SSM_STATE), act(COL_SC, SSM_STATE),
                  pl.BlockSpec((1, 1, e, rows), lambda bi, gi, t: (bi, gi, 0, t)),
                  col, col, grp(2 * LANES, SEL_WIDTH),
                  grp(SSM_CONV, conv_width), grp(1, conv_width), par, par],
        out_specs=pl.BlockSpec((1, rows, SSM_GROUP_WIDTH), lambda bi, gi, t: (bi, t, gi)),
        out_shape=jax.ShapeDtypeStruct((b, s, SSM_WIDTH), BF16),
        scratch_shapes=[pltpu.VMEM((SSM_STATE, SSM_GROUP_WIDTH), F32),
                        pltpu.VMEM((SUBLANES + rows, conv_width), F32),
                        pltpu.VMEM((rows // SSD_CHUNK, e, SSD_CHUNK), F32)],
        compiler_params=_params("parallel", "parallel", "arbitrary"),
        name="ssd",
    )(proj3, proj3, proj3, proj3, ac_row, ac_col, dt_col, sel,
      per_group(conv_w), per_group(conv_b.reshape(1, -1)), dsk, norm_w.reshape(1, SSM_WIDTH))


def _layer(x2, batch, cos, sin, norm_w, w_in, conv_w, conv_b, dt_bias, a_log, d_skip,
           hgrn_norm_w, ssm_norm_w, w_out, lb):
    m, d = x2.shape
    s = m // batch
    h = _rmsnorm(x2, norm_w, BF16)
    w_main = w_in[:, :IN_MAIN].astype(BF16)
    w_dt = jnp.pad(w_in[:, IN_MAIN:], ((0, 0), (0, LANES - SSM_HEADS))).astype(BF16)
    proj, dt = _inproj(h, w_main, w_dt)
    proj3 = proj.reshape(batch, s, IN_MAIN)
    dt3 = dt.reshape(batch, s, LANES)
    q_rot, k_pad, v_pad = _rope_apply(proj3, cos, sin)
    ya = _attention(q_rot, k_pad, v_pad, proj3)
    yh = _hgrn(proj3, lb, hgrn_norm_w)
    ys = _ssd(proj3, dt3, conv_w, conv_b, dt_bias, a_log, d_skip, ssm_norm_w)
    return _outproj(ya.reshape(m, -1), yh.reshape(m, -1), ys.reshape(m, -1),
                    w_out.astype(BF16), x2)


def kernel(x, positions, norm_w, w_in, conv_w, conv_b, dt_bias, a_log, d_skip, hgrn_norm_w,
           ssm_norm_w, w_out, hgrn_lb_logits, final_norm_w):
    batch, s, d = x.shape
    p = jax.nn.softmax(hgrn_lb_logits.astype(F32), axis=0)
    cs = jnp.cumsum(p, axis=0)
    lb_all = cs - cs[0:1]
    cos, sin = _rope_tables(positions)
    x2 = x.reshape(batch * s, d)
    for l in range(DEPTH):
        x2 = _layer(x2, batch, cos, sin, norm_w[l], w_in[l], conv_w[l], conv_b[l], dt_bias[l],
                    a_log[l], d_skip[l], hgrn_norm_w[l], ssm_norm_w[l], w_out[l], lb_all[l])
    return _rmsnorm(x2, final_norm_w, F32).reshape(batch, s, d)
```

```python
import functools

import numpy as np
import jax
import jax.numpy as jnp
from jax import lax
from jax.experimental import pallas as pl
from jax.experimental.pallas import tpu as pltpu

F32 = jnp.float32
BF16 = jnp.bfloat16

D_MODEL = 4096
DEPTH = 2
NORM_EPS = 1e-6
ATTN_HEADS = 8
ATTN_HEAD_DIM = 128
ATTN_WIDTH = ATTN_HEADS * ATTN_HEAD_DIM
ATTN_PATTERNS = ((128, 1), (512, 4), (2048, 16))
ATTN_REACH = max(w for w, _ in ATTN_PATTERNS)
ROPE_THETA = 10000.0
HGRN_HEADS = 8
HGRN_DIM = 128
HGRN_WIDTH = HGRN_HEADS * HGRN_DIM
SSM_HEADS = 32
SSM_HEAD_DIM = 64
SSM_WIDTH = SSM_HEADS * SSM_HEAD_DIM
SSM_GROUPS = 8
SSM_HEADS_PER_GROUP = SSM_HEADS // SSM_GROUPS
SSM_GROUP_WIDTH = SSM_WIDTH // SSM_GROUPS
SSM_STATE = 128
SSM_CONV = 4
MIX_WIDTH = ATTN_WIDTH + HGRN_WIDTH + SSM_WIDTH

COL_AQ, COL_AK, COL_AV, COL_AG = 0, 1024, 2048, 3072
COL_HQ, COL_HF, COL_HI, COL_HG = 4096, 5120, 6144, 7168
COL_SZ = 8192
COL_SX = 10240
COL_SB = COL_SX + SSM_WIDTH
COL_SC = COL_SB + SSM_GROUPS * SSM_STATE
COL_DT = COL_SC + SSM_GROUPS * SSM_STATE
IN_MAIN = COL_DT

LANES = 128
SUBLANES = 8
VMEM_LIMIT = 56 * 1024 * 1024

NEG = -1e30
LOG2E = 1.4426950408889634

NORM_ROWS = 512
PROJ_TM, PROJ_TN = 1024, 1024
OUT_TM, OUT_TN = 512, 1024
CAST_ROWS = 256
NORM_CHUNK = 32
ROPE_ROWS = 512
ATTN_TQ = 256
ATTN_HEADS_PER_STEP = 2
HGRN_ROWS = 512
HGRN_CHUNK = 128
HGRN_HEADS_PER_STEP = 4
SSD_ROWS = 512
SSD_CHUNK = 128
SSD_DT_ROWS = 1024


def _params(*sem):
    return pltpu.CompilerParams(dimension_semantics=sem, vmem_limit_bytes=VMEM_LIMIT)


def _rmsnorm_dt_kernel(x_ref, w_ref, wdt_ref, h_ref, dt_ref):
    x = x_ref[...]
    ms = jnp.mean(x * x, axis=-1, keepdims=True)
    h = ((x * lax.rsqrt(ms + NORM_EPS)) * w_ref[...]).astype(BF16)
    h_ref[...] = h
    dt_ref[...] = jnp.dot(h, wdt_ref[...], preferred_element_type=F32)


def _rmsnorm_dt(x2, w, w_dt):
    m, d = x2.shape
    rows = pl.BlockSpec((NORM_ROWS, d), lambda i: (i, 0))
    return pl.pallas_call(
        _rmsnorm_dt_kernel,
        grid=(m // NORM_ROWS,),
        in_specs=[rows, pl.BlockSpec((1, d), lambda i: (0, 0)),
                  pl.BlockSpec((d, LANES), lambda i: (0, 0))],
        out_specs=[rows, pl.BlockSpec((NORM_ROWS, LANES), lambda i: (i, 0))],
        out_shape=[jax.ShapeDtypeStruct((m, d), BF16), jax.ShapeDtypeStruct((m, LANES), F32)],
        compiler_params=_params("parallel"),
        name="rmsnorm_dt",
    )(x2, w.reshape(1, d), w_dt)


def _inproj_kernel(h_ref, w_ref, o_ref, wbf_ref):
    @pl.when(pl.program_id(1) == 0)
    def _():
        def cast_rows(r, carry):
            rows = pl.ds(pl.multiple_of(r * CAST_ROWS, CAST_ROWS), CAST_ROWS)
            wbf_ref[rows, :] = w_ref[0, rows, :].astype(BF16)
            return carry

        lax.fori_loop(0, wbf_ref.shape[0] // CAST_ROWS, cast_rows, 0)

    o_ref[...] = jnp.dot(h_ref[...], wbf_ref[...], preferred_element_type=F32)


def _inproj(h, w_in, layer):
    m, d = h.shape
    return pl.pallas_call(
        _inproj_kernel,
        grid=(IN_MAIN // PROJ_TN, m // PROJ_TM),
        in_specs=[pl.BlockSpec((PROJ_TM, d), lambda j, i: (i, 0)),
                  pl.BlockSpec((1, d, PROJ_TN), lambda j, i: (layer, 0, j),
                               pipeline_mode=pl.Buffered(1))],
        out_specs=pl.BlockSpec((PROJ_TM, PROJ_TN), lambda j, i: (i, j)),
        out_shape=jax.ShapeDtypeStruct((m, IN_MAIN), F32),
        scratch_shapes=[pltpu.VMEM((d, PROJ_TN), BF16)],
        compiler_params=_params("parallel", "arbitrary"),
        name="inproj",
    )(h, w_in)


def _outproj_kernel(final, ya_ref, yh_ref, ys_ref, wa_ref, wh_ref, ws_ref, x_ref, nw_ref, *rest):
    if final:
        out_ref, slab_ref = rest
    else:
        wdt_ref, xo_ref, out_ref, dt_ref, slab_ref = rest
    j = pl.program_id(1)
    n_slabs, _, tn = slab_ref.shape
    acc = jnp.dot(ya_ref[...], wa_ref[...], preferred_element_type=F32)
    acc += jnp.dot(yh_ref[...], wh_ref[...], preferred_element_type=F32)
    acc += jnp.dot(ys_ref[...], ws_ref[...], preferred_element_type=F32)
    x_new = x_ref[...] + acc
    slab_ref[j] = x_new
    if not final:
        xo_ref[...] = x_new

    @pl.when(j == n_slabs - 1)
    def _():
        def norm_rows(r, carry):
            rows = pl.ds(pl.multiple_of(r * NORM_CHUNK, NORM_CHUNK), NORM_CHUNK)
            ss = jnp.sum(slab_ref[0, rows, :] * slab_ref[0, rows, :], axis=-1, keepdims=True)
            for s in range(1, n_slabs):
                ss = ss + jnp.sum(slab_ref[s, rows, :] * slab_ref[s, rows, :], axis=-1, keepdims=True)
            scale = lax.rsqrt(ss / (n_slabs * tn) + NORM_EPS)
            for s in range(n_slabs):
                cols = slice(s * tn, (s + 1) * tn)
                out_ref[rows, cols] = ((slab_ref[s, rows, :] * scale)
                                       * nw_ref[:, cols]).astype(out_ref.dtype)
            return carry

        lax.fori_loop(0, slab_ref.shape[1] // NORM_CHUNK, norm_rows, 0)
        if not final:
            dt_ref[...] = jnp.dot(out_ref[...], wdt_ref[...], preferred_element_type=F32)


def _outproj(ya, yh, ys, w_out, x2, norm_w, w_dt=None):
    m, d = x2.shape
    final = w_dt is None
    row = lambda i, j: (i, 0)
    col = lambda i, j: (0, j)
    tile = lambda i, j: (i, j)
    const = lambda i, j: (0, 0)
    assert HGRN_WIDTH == ATTN_WIDTH and SSM_WIDTH == ATTN_WIDTH + HGRN_WIDTH
    in_specs = [pl.BlockSpec((OUT_TM, ATTN_WIDTH), row),
                pl.BlockSpec((OUT_TM, HGRN_WIDTH), row),
                pl.BlockSpec((OUT_TM, SSM_WIDTH), row),
                pl.BlockSpec((ATTN_WIDTH, OUT_TN), col),
                pl.BlockSpec((HGRN_WIDTH, OUT_TN), lambda i, j: (1, j)),
                pl.BlockSpec((SSM_WIDTH, OUT_TN), lambda i, j: (1, j)),
                pl.BlockSpec((OUT_TM, OUT_TN), tile),
                pl.BlockSpec((1, d), const)]
    args = [ya, yh, ys, w_out, w_out, w_out, x2, norm_w.reshape(1, d)]
    full_rows = pl.BlockSpec((OUT_TM, d), row)
    if final:
        out_specs = full_rows
        out_shape = jax.ShapeDtypeStruct((m, d), F32)
    else:
        in_specs.append(pl.BlockSpec((d, LANES), const))
        args.append(w_dt)
        out_specs = [pl.BlockSpec((OUT_TM, OUT_TN), tile), full_rows,
                     pl.BlockSpec((OUT_TM, LANES), row)]
        out_shape = [jax.ShapeDtypeStruct((m, d), F32), jax.ShapeDtypeStruct((m, d), BF16),
                     jax.ShapeDtypeStruct((m, LANES), F32)]
    return pl.pallas_call(
        functools.partial(_outproj_kernel, final),
        grid=(m // OUT_TM, d // OUT_TN),
        in_specs=in_specs,
        out_specs=out_specs,
        out_shape=out_shape,
        scratch_shapes=[pltpu.VMEM((d // OUT_TN, OUT_TM, OUT_TN), F32)],
        compiler_params=_params("parallel", "arbitrary"),
        name="outproj_final" if final else "outproj",
    )(*args)


def _rope_table_kernel(pos_ref, invf_ref, cos_ref, sin_ref):
    ang = pos_ref[0] * invf_ref[...]
    lane = lax.broadcasted_iota(jnp.int32, ang.shape, 1)
    cos_ref[0] = jnp.cos(ang)
    s = jnp.sin(ang)
    sin_ref[0] = jnp.where(lane < ATTN_HEAD_DIM // 2, -s, s)


def _rope_tables(positions):
    b, s = positions.shape
    half = ATTN_HEAD_DIM // 2
    inv_freq = ROPE_THETA ** (-jnp.arange(half, dtype=F32) / half)
    invf = jnp.concatenate([inv_freq, inv_freq]).reshape(1, ATTN_HEAD_DIM)
    posb = jnp.broadcast_to(positions.astype(F32)[..., None], (b, s, ATTN_HEAD_DIM))
    spec = pl.BlockSpec((1, ROPE_ROWS, ATTN_HEAD_DIM), lambda bi, i: (bi, i, 0))
    return pl.pallas_call(
        _rope_table_kernel,
        grid=(b, s // ROPE_ROWS),
        in_specs=[spec, pl.BlockSpec((1, ATTN_HEAD_DIM), lambda bi, i: (0, 0))],
        out_specs=[spec, spec],
        out_shape=[jax.ShapeDtypeStruct((b, s, ATTN_HEAD_DIM), F32)] * 2,
        compiler_params=_params("parallel", "parallel"),
        name="rope_tables",
    )(posb, invf)


def _rope_kernel(pad_blocks, q_ref, k_ref, v_ref, cos_ref, sin_ref, qo_ref, ko_ref, vo_ref):
    i = pl.program_id(1)
    cos = cos_ref[0]
    sin = sin_ref[0]
    half = ATTN_HEAD_DIM // 2
    scale = ATTN_HEAD_DIM ** -0.5 * LOG2E

    def rope(t):
        return t * cos + pltpu.roll(t, half, axis=1) * sin

    for h in range(ATTN_HEADS):
        sl = slice(h * ATTN_HEAD_DIM, (h + 1) * ATTN_HEAD_DIM)
        qo_ref[0, :, sl] = (rope(q_ref[0, :, sl]) * scale).astype(BF16)

    @pl.when(i < pad_blocks)
    def _():
        ko_ref[...] = jnp.zeros_like(ko_ref)
        vo_ref[...] = jnp.zeros_like(vo_ref)

    @pl.when(i >= pad_blocks)
    def _():
        for h in range(ATTN_HEADS):
            sl = slice(h * ATTN_HEAD_DIM, (h + 1) * ATTN_HEAD_DIM)
            ko_ref[0, :, sl] = rope(k_ref[0, :, sl]).astype(BF16)
        vo_ref[0] = v_ref[0].astype(BF16)


def _rope_apply(proj3, cos, sin):
    b, s, _ = proj3.shape
    pad_blocks = ATTN_REACH // ROPE_ROWS
    nblk = s // ROPE_ROWS + pad_blocks
    src = lambda c: (lambda bi, i: (bi, jnp.maximum(i - pad_blocks, 0), c))
    slab = lambda c: pl.BlockSpec((1, ROPE_ROWS, ATTN_WIDTH), src(c))
    tab = pl.BlockSpec((1, ROPE_ROWS, ATTN_HEAD_DIM), src(0))
    return pl.pallas_call(
        functools.partial(_rope_kernel, pad_blocks),
        grid=(b, nblk),
        in_specs=[slab(COL_AQ // ATTN_WIDTH), slab(COL_AK // ATTN_WIDTH), slab(COL_AV // ATTN_WIDTH),
                  tab, tab],
        out_specs=[pl.BlockSpec((1, ROPE_ROWS, ATTN_WIDTH), src(0)),
                   pl.BlockSpec((1, ROPE_ROWS, ATTN_WIDTH), lambda bi, i: (bi, i, 0)),
                   pl.BlockSpec((1, ROPE_ROWS, ATTN_WIDTH), lambda bi, i: (bi, i, 0))],
        out_shape=[jax.ShapeDtypeStruct((b, s, ATTN_WIDTH), BF16),
                   jax.ShapeDtypeStruct((b, s + ATTN_REACH, ATTN_WIDTH), BF16),
                   jax.ShapeDtypeStruct((b, s + ATTN_REACH, ATTN_WIDTH), BF16)],
        compiler_params=_params("parallel", "arbitrary"),
        name="rope_apply",
    )(proj3, proj3, proj3, cos, sin)


def _attn_bias_table(tq):
    r = np.arange(tq)[:, None]
    j = np.arange(ATTN_REACH + tq)[None, :]
    delta = r + ATTN_REACH - j
    count = np.zeros(delta.shape, np.int64)
    for window, dilation in ATTN_PATTERNS:
        count += (delta >= 0) & (delta % dilation == 0) & (delta <= window)
    return np.where(count > 0, np.log2(np.maximum(count, 1)), NEG).astype(np.float32)


def _attn_kernel(q_ref, k_ref, v_ref, g_ref, bias_ref, o_ref):
    i = pl.program_id(2)
    tq = q_ref.shape[1]
    span = ATTN_REACH + tq
    start = pl.multiple_of(i * tq, tq)
    keys = pl.ds(start, span)
    col = lax.broadcasted_iota(jnp.int32, (1, span), 1)
    in_seq = col >= ATTN_REACH - i * tq
    for hh in range(ATTN_HEADS_PER_STEP):
        hs = slice(hh * ATTN_HEAD_DIM, (hh + 1) * ATTN_HEAD_DIM)
        q = q_ref[0, :, hs]
        s = lax.dot_general(q, k_ref[0, keys, hs], (((1,), (1,)), ((), ())),
                            preferred_element_type=F32)
        s = jnp.where(in_seq, s + bias_ref[...], NEG)
        m = jnp.max(s, axis=-1, keepdims=True)
        p = jnp.exp2(s - m)
        den = jnp.sum(p, axis=-1, keepdims=True)
        o = jnp.dot(p.astype(BF16), v_ref[0, keys, hs], preferred_element_type=F32) / den
        g = g_ref[0, :, hs]
        o_ref[0, :, hs] = (o * (g * jax.nn.sigmoid(g))).astype(o_ref.dtype)


def _attention(q_rot, k_pad, v_pad, proj3):
    b, s, _ = q_rot.shape
    tq = ATTN_TQ
    width = ATTN_HEADS_PER_STEP * ATTN_HEAD_DIM
    bias = jnp.asarray(_attn_bias_table(tq))
    gate0 = COL_AG // width
    full = pl.BlockSpec((1, s + ATTN_REACH, width), lambda bi, h, i: (bi, 0, h))
    return pl.pallas_call(
        _attn_kernel,
        grid=(b, ATTN_HEADS // ATTN_HEADS_PER_STEP, s // tq),
        in_specs=[pl.BlockSpec((1, tq, width), lambda bi, h, i: (bi, i, h)),
                  full, full,
                  pl.BlockSpec((1, tq, width), lambda bi, h, i: (bi, i, gate0 + h)),
                  pl.BlockSpec(bias.shape, lambda bi, h, i: (0, 0))],
        out_specs=pl.BlockSpec((1, tq, width), lambda bi, h, i: (bi, i, h)),
        out_shape=jax.ShapeDtypeStruct((b, s, ATTN_WIDTH), BF16),
        compiler_params=_params("parallel", "parallel", "arbitrary"),
        name="attention",
    )(q_rot, k_pad, v_pad, proj3, bias)


def _split3(x):
    hi = x.astype(BF16)
    r1 = x - hi.astype(F32)
    mid = r1.astype(BF16)
    lo = (r1 - mid.astype(F32)).astype(BF16)
    return hi, mid, lo


def _cumsum_rows_mxu(x, tril):
    n = x.shape[1]
    r = jnp.dot(tril, jnp.concatenate(_split3(x), axis=1), preferred_element_type=F32)
    return (r[:, :n] + r[:, n:2 * n]) + r[:, 2 * n:]


def _log1p_exp_neg_abs(x):
    return jnp.log(1.0 + jnp.exp(-jnp.abs(x)))


def _hgrn_pair_level(c):
    t = lax.broadcasted_iota(jnp.int32, (c, c), 0)
    s = lax.broadcasted_iota(jnp.int32, (c, c), 1)
    x = t ^ s
    level = jnp.full((c, c), -1, jnp.int32)
    m = 1
    while m < c:
        level = level + (x >= m).astype(jnp.int32)
        m *= 2
    return jnp.where(t >= s, level, -2)


def _hgrn_chunk(q, fr, v, lb, log_lb, log1m_lb, st, tril, pair_level):
    c = q.shape[0]
    nt = (((1,), (1,)), ((), ()))
    tn = (((0,), (0,)), ((), ()))
    log_sig = jnp.minimum(fr, 0.0) - _log1p_exp_neg_abs(fr)
    lower = log1m_lb + log_sig
    g = jnp.maximum(log_lb, lower) + _log1p_exp_neg_abs(log_lb - lower)
    kk = (1.0 - lb) * jax.nn.sigmoid(-fr)
    b = _cumsum_rows_mxu(g, tril)
    b_last = b[c - 1:c, :]

    o = lax.dot_general((q * jnp.exp(b)).astype(BF16), st.astype(BF16), nt,
                        preferred_element_type=F32)

    n = q.shape[1]
    a = jnp.where(pair_level == -1,
                  lax.dot_general(q.astype(BF16), kk.astype(BF16), nt, preferred_element_type=F32), 0.0)
    b3 = b.reshape(c // SUBLANES, SUBLANES, n)
    sub3 = lax.broadcasted_iota(jnp.int32, b3.shape, 1)
    level, m = 0, 1
    while m < c:
        if m >= SUBLANES:
            mid = jnp.concatenate(
                [jnp.broadcast_to(b[blk * 2 * m + m - 1:blk * 2 * m + m, :], (2 * m, n))
                 for blk in range(c // (2 * m))], axis=0)
        else:
            mid3 = jnp.broadcast_to(b3[:, m - 1:m, :], b3.shape)
            for first in range(2 * m, SUBLANES, 2 * m):
                mid3 = jnp.where(sub3 >= first,
                                 jnp.broadcast_to(b3[:, first + m - 1:first + m, :], b3.shape), mid3)
            mid = mid3.reshape(c, n)
        e = jnp.exp(-jnp.abs(b - mid))
        part = lax.dot_general((q * e).astype(BF16), (kk * e).astype(BF16), nt,
                               preferred_element_type=F32)
        a = jnp.where(pair_level == level, part, a)
        level, m = level + 1, 2 * m
    o = o + jnp.dot(a.astype(BF16), v.astype(BF16), preferred_element_type=F32)

    kd = kk * jnp.exp(b_last - b)
    st_new = st * jnp.exp(b_last) + lax.dot_general(v.astype(BF16), kd.astype(BF16), tn,
                                                    preferred_element_type=F32)
    return o, st_new


def _hgrn_kernel(q_ref, f_ref, i_ref, g_ref, lb_ref, llb_ref, l1m_ref, nw_ref, o_ref, st_ref):
    @pl.when(pl.program_id(2) == 0)
    def _():
        st_ref[...] = jnp.zeros_like(st_ref)

    rows = q_ref.shape[1]
    ri = lax.broadcasted_iota(jnp.int32, (HGRN_CHUNK, HGRN_CHUNK), 0)
    ci_ = lax.broadcasted_iota(jnp.int32, (HGRN_CHUNK, HGRN_CHUNK), 1)
    tril = (ri >= ci_).astype(BF16)
    pair_level = _hgrn_pair_level(HGRN_CHUNK)

    def body(ci, carry):
        r0 = pl.multiple_of(ci * HGRN_CHUNK, HGRN_CHUNK)
        sl = pl.ds(r0, HGRN_CHUNK)
        for hh in range(HGRN_HEADS_PER_STEP):
            hs = slice(hh * HGRN_DIM, (hh + 1) * HGRN_DIM)
            o, st = _hgrn_chunk(q_ref[0, sl, hs], f_ref[0, sl, hs], i_ref[0, sl, hs],
                                lb_ref[:, hs], llb_ref[:, hs], l1m_ref[:, hs], st_ref[hh], tril,
                                pair_level)
            st_ref[hh] = st
            o = o * lax.rsqrt(jnp.mean(o * o, axis=-1, keepdims=True) + NORM_EPS) * nw_ref[:, hs]
            g = g_ref[0, sl, hs]
            o_ref[0, sl, hs] = (o * (g * jax.nn.sigmoid(g))).astype(o_ref.dtype)
        return carry

    lax.fori_loop(0, rows // HGRN_CHUNK, body, 0)


def _hgrn(proj3, lb, norm_w):
    b, s, _ = proj3.shape
    lb = lb.reshape(1, HGRN_WIDTH)
    log_lb = jnp.log(lb)
    log1m_lb = jnp.log1p(-lb)
    width = HGRN_HEADS_PER_STEP * HGRN_DIM
    blk = lambda c0: pl.BlockSpec((1, HGRN_ROWS, width), lambda bi, h, t: (bi, t, c0 // width + h))
    par = pl.BlockSpec((1, width), lambda bi, h, t: (0, h))
    return pl.pallas_call(
        _hgrn_kernel,
        grid=(b, HGRN_HEADS // HGRN_HEADS_PER_STEP, s // HGRN_ROWS),
        in_specs=[blk(COL_HQ), blk(COL_HF), blk(COL_HI), blk(COL_HG), par, par, par, par],
        out_specs=pl.BlockSpec((1, HGRN_ROWS, width), lambda bi, h, t: (bi, t, h)),
        out_shape=jax.ShapeDtypeStruct((b, s, HGRN_WIDTH), BF16),
        scratch_shapes=[pltpu.VMEM((HGRN_HEADS_PER_STEP, HGRN_DIM, HGRN_DIM), F32)],
        compiler_params=_params("parallel", "parallel", "arbitrary"),
        name="hgrn2",
    )(proj3, proj3, proj3, proj3, lb, log_lb, log1m_lb, norm_w.reshape(1, HGRN_WIDTH))


def _softplus(x):
    return jnp.maximum(x, 0.0) + _log1p_exp_neg_abs(x)


def _silu(x):
    return x * jax.nn.sigmoid(x)


def _cumsum_lanes(x):
    n = x.shape[1]
    lane = lax.broadcasted_iota(jnp.int32, x.shape, 1)
    k = 1
    while k < n:
        x = x + jnp.where(lane >= k, pltpu.roll(x, k, axis=1), 0.0)
        k *= 2
    return x


def _ssd_dt_kernel(dt_ref, bias_ref, alog_ref, acrow_ref, accol_ref, dtcol_ref):
    heads, lc = SSM_HEADS, SSD_CHUNK
    neg_a = jnp.exp(alog_ref[...])
    zeros = jnp.zeros((LANES - 3 * heads, lc), F32)
    for c in range(dt_ref.shape[1] // lc):
        sl = slice(c * lc, (c + 1) * lc)
        raw = dt_ref[0, sl, :].T[:heads]
        dt = _softplus(raw + bias_ref[...])
        ac = _cumsum_lanes(dt * (-neg_a))
        acrow_ref[0, :, sl] = ac
        for val, dst in ((ac, accol_ref), (dt, dtcol_ref)):
            pieces = [p.astype(F32) for p in _split3(val)]
            dst[0, sl, :] = jnp.concatenate(pieces + [zeros], axis=0).T.astype(BF16)


def _ssd_dt(dt3, dt_bias, a_log):
    b, s, _ = dt3.shape
    rows = SSD_DT_ROWS
    col = pl.BlockSpec((1, rows, LANES), lambda bi, t: (bi, t, 0))
    par = pl.BlockSpec((SSM_HEADS, 1), lambda bi, t: (0, 0))
    return pl.pallas_call(
        _ssd_dt_kernel,
        grid=(b, s // rows),
        in_specs=[col, par, par],
        out_specs=[pl.BlockSpec((1, SSM_HEADS, rows), lambda bi, t: (bi, 0, t)), col, col],
        out_shape=[jax.ShapeDtypeStruct((b, SSM_HEADS, s), F32),
                   jax.ShapeDtypeStruct((b, s, LANES), BF16),
                   jax.ShapeDtypeStruct((b, s, LANES), BF16)],
        compiler_params=_params("parallel", "parallel"),
        name="ssd_dt",
    )(dt3, dt_bias.reshape(SSM_HEADS, 1), a_log.reshape(SSM_HEADS, 1))


SEL_AC_WIDE = SSM_HEADS_PER_GROUP * LANES
SEL_AC_X = SEL_AC_WIDE + SSM_GROUP_WIDTH
SEL_WIDTH = SEL_AC_X + SSM_GROUP_WIDTH


def _ssd_select_matrices():
    sel = np.zeros((SSM_GROUPS, 2 * LANES, SEL_WIDTH), np.float32)
    for g in range(SSM_GROUPS):
        for e in range(SSM_HEADS_PER_GROUP):
            head = g * SSM_HEADS_PER_GROUP + e
            for piece in range(3):
                r = piece * SSM_HEADS + head
                sel[g, r, e * LANES:(e + 1) * LANES] = 1.0
                sel[g, r, SEL_AC_WIDE + e * SSM_HEAD_DIM:SEL_AC_WIDE + (e + 1) * SSM_HEAD_DIM] = 1.0
                sel[g, LANES + r, SEL_AC_X + e * SSM_HEAD_DIM:SEL_AC_X + (e + 1) * SSM_HEAD_DIM] = 1.0
    return sel


def _ssd_kernel(z_ref, x_ref, b_ref, c_ref, acrow_ref, accol_ref, dtcol_ref, sel_ref,
                cw_ref, cb_ref, dsk_ref, nw_ref, o_ref, st_ref, stage_ref, xbc_ref, acr_ref):
    e_heads, p_dim, lc = SSM_HEADS_PER_GROUP, SSM_HEAD_DIM, SSD_CHUNK
    rows = x_ref.shape[1]
    xw, sw = SSM_GROUP_WIDTH, SSM_STATE
    n_slabs = (xw + 2 * sw) // LANES

    @pl.when(pl.program_id(2) == 0)
    def _():
        st_ref[...] = jnp.zeros_like(st_ref)
        stage_ref[:, :SUBLANES, :] = jnp.zeros((n_slabs, SUBLANES, LANES), F32)

    for slab in range(xw // LANES):
        stage_ref[slab, SUBLANES:, :] = x_ref[0, :, slab * LANES:(slab + 1) * LANES]
    stage_ref[xw // LANES, SUBLANES:, :] = b_ref[0]
    stage_ref[xw // LANES + 1, SUBLANES:, :] = c_ref[0]
    for ci in range(rows // lc):
        acr_ref[ci] = acrow_ref[0, 0, :, ci * lc:(ci + 1) * lc]

    li = lax.broadcasted_iota(jnp.int32, (lc, lc), 0)
    si = lax.broadcasted_iota(jnp.int32, (lc, lc), 1)
    causal = li >= si
    lane_head = lax.broadcasted_iota(jnp.int32, (1, xw), 1) // p_dim
    nt = (((1,), (1,)), ((), ()))
    tn = (((0,), (0,)), ((), ()))
    cw = cw_ref[0]
    cbias = cb_ref[0]

    def body(ci, carry):
        r0 = pl.multiple_of(ci * lc, lc)
        sl = pl.ds(r0, lc)
        for slab in range(n_slabs):
            lanes = slice(slab * LANES, (slab + 1) * LANES)
            for parity in range(2):
                acc = cbias[:, lanes]
                for j in range(SSM_CONV):
                    first = r0 + (SUBLANES - (SSM_CONV - 1) + j + parity)
                    tap = stage_ref[slab, pl.ds(first, lc // 2, stride=2), :]
                    acc = acc + tap * cw[j:j + 1, lanes]
                xbc_ref[slab, pl.ds(r0 + parity, lc // 2, stride=2), :] = _silu(acc)
        xs = jnp.concatenate([xbc_ref[slab, sl, :] for slab in range(xw // LANES)], axis=1)
        bm = xbc_ref[xw // LANES, sl, :].astype(BF16)
        cm = xbc_ref[xw // LANES + 1, sl, :].astype(BF16)

        pieces = jnp.concatenate([accol_ref[0, sl, :], dtcol_ref[0, sl, :]], axis=1)
        bc = jnp.dot(pieces, sel_ref[0], preferred_element_type=F32)
        ac_x = bc[:, SEL_AC_WIDE:SEL_AC_X]
        xdt = xs * bc[:, SEL_AC_X:]
        ac_last = ac_x[lc - 1:lc, :]
        ac_r = acr_ref[ci]

        cb = lax.dot_general(cm, bm, nt, preferred_element_type=F32)
        decayed, x_heads = [], []
        for e in range(e_heads):
            dif = bc[:, e * LANES:(e + 1) * LANES] - ac_r[e:e + 1, :]
            decayed.append((cb * jnp.exp(jnp.where(causal, dif, NEG))).astype(BF16))
            x_heads.append(jnp.where(lane_head == e, xdt, 0.0).astype(BF16))
        y = jnp.dot(jnp.concatenate(decayed, axis=1), jnp.concatenate(x_heads, axis=0),
                    preferred_element_type=F32)
        st = st_ref[...]
        y = y + jnp.dot(cm, st.astype(BF16), preferred_element_type=F32) * jnp.exp(ac_x)
        xdec = (xdt * jnp.exp(ac_last - ac_x)).astype(BF16)
        st_ref[...] = st * jnp.exp(ac_last) + lax.dot_general(bm, xdec, tn,
                                                              preferred_element_type=F32)
        y = y + xs * dsk_ref[...]
        y = y * _silu(z_ref[0, sl, :])
        y = y * lax.rsqrt(jnp.mean(y * y, axis=-1, keepdims=True) + NORM_EPS) * nw_ref[...]
        o_ref[0, sl, :] = y.astype(o_ref.dtype)
        return carry

    lax.fori_loop(0, rows // lc, body, 0)
    stage_ref[:, :SUBLANES, :] = stage_ref[:, rows:rows + SUBLANES, :]


def _ssd(proj3, dt3, conv_w, conv_b, dt_bias, a_log, d_skip, norm_w):
    b, s, _ = proj3.shape
    g, e = SSM_GROUPS, SSM_HEADS_PER_GROUP
    rows = SSD_ROWS
    ac_row, ac_col, dt_col = _ssd_dt(dt3, dt_bias, a_log)
    ac_row = ac_row.reshape(b, g, e, s)
    sel = jnp.asarray(_ssd_select_matrices(), BF16)

    def per_group(p):
        px = p[:, :SSM_WIDTH].reshape(-1, g, SSM_GROUP_WIDTH)
        pb = p[:, SSM_WIDTH:SSM_WIDTH + g * SSM_STATE].reshape(-1, g, SSM_STATE)
        pc = p[:, SSM_WIDTH + g * SSM_STATE:].reshape(-1, g, SSM_STATE)
        return jnp.concatenate([px, pb, pc], axis=2).transpose(1, 0, 2)

    conv_width = SSM_GROUP_WIDTH + 2 * SSM_STATE
    dsk = jnp.repeat(d_skip, SSM_HEAD_DIM).reshape(1, SSM_WIDTH)

    def act(c0, width):
        return pl.BlockSpec((1, rows, width), lambda bi, gi, t: (bi, t, c0 // width + gi))

    col = pl.BlockSpec((1, rows, LANES), lambda bi, gi, t: (bi, t, 0))
    grp = lambda nrows, width: pl.BlockSpec((1, nrows, width), lambda bi, gi, t: (gi, 0, 0))
    par = pl.BlockSpec((1, SSM_GROUP_WIDTH), lambda bi, gi, t: (0, gi))
    return pl.pallas_call(
        _ssd_kernel,
        grid=(b, g, s // rows),
        in_specs=[act(COL_SZ, SSM_GROUP_WIDTH), act(COL_SX, SSM_GROUP_WIDTH),
                  act(COL_SB, SSM_STATE), act(COL_SC, SSM_STATE),
                  pl.BlockSpec((1, 1, e, rows), lambda bi, gi, t: (bi, gi, 0, t)),
                  col, col, grp(2 * LANES, SEL_WIDTH),
                  grp(SSM_CONV, conv_width), grp(1, conv_width), par, par],
        out_specs=pl.BlockSpec((1, rows, SSM_GROUP_WIDTH), lambda bi, gi, t: (bi, t, gi)),
        out_shape=jax.ShapeDtypeStruct((b, s, SSM_WIDTH), BF16),
        scratch_shapes=[pltpu.VMEM((SSM_STATE, SSM_GROUP_WIDTH), F32),
                        pltpu.VMEM((conv_width // LANES, SUBLANES + rows, LANES), F32),
                        pltpu.VMEM((conv_width // LANES, rows, LANES), F32),
                        pltpu.VMEM((rows // SSD_CHUNK, e, SSD_CHUNK), F32)],
        compiler_params=_params("parallel", "parallel", "arbitrary"),
        name="ssd",
    )(proj3, proj3, proj3, proj3, ac_row, ac_col, dt_col, sel,
      per_group(conv_w), per_group(conv_b.reshape(1, -1)), dsk, norm_w.reshape(1, SSM_WIDTH))


def _mixers(h, dt, batch, cos, sin, w_in, layer, conv_w, conv_b, dt_bias, a_log, d_skip,
            hgrn_norm_w, ssm_norm_w, lb):
    m = h.shape[0]
    s = m // batch
    proj3 = _inproj(h, w_in, layer).reshape(batch, s, IN_MAIN)
    dt3 = dt.reshape(batch, s, LANES)
    q_rot, k_pad, v_pad = _rope_apply(proj3, cos, sin)
    ya = _attention(q_rot, k_pad, v_pad, proj3)
    yh = _hgrn(proj3, lb, hgrn_norm_w)
    ys = _ssd(proj3, dt3, conv_w, conv_b, dt_bias, a_log, d_skip, ssm_norm_w)
    return ya.reshape(m, -1), yh.reshape(m, -1), ys.reshape(m, -1)


def kernel(x, positions, norm_w, w_in, conv_w, conv_b, dt_bias, a_log, d_skip, hgrn_norm_w,
           ssm_norm_w, w_out, hgrn_lb_logits, final_norm_w):
    batch, s, d = x.shape
    p = jax.nn.softmax(hgrn_lb_logits.astype(F32), axis=0)
    cs = jnp.cumsum(p, axis=0)
    lb_all = cs - cs[0:1]
    w_dt = jnp.pad(w_in[:, :, IN_MAIN:], ((0, 0), (0, 0), (0, LANES - SSM_HEADS))).astype(BF16)
    cos, sin = _rope_tables(positions)
    x2 = x.reshape(batch * s, d)
    h, dt = _rmsnorm_dt(x2, norm_w[0], w_dt[0])
    for l in range(DEPTH):
        ya, yh, ys = _mixers(h, dt, batch, cos, sin, w_in, l, conv_w[l], conv_b[l], dt_bias[l],
                             a_log[l], d_skip[l], hgrn_norm_w[l], ssm_norm_w[l], lb_all[l])
        w_out_l = w_out[l].astype(BF16)
        if l + 1 < DEPTH:
            x2, h, dt = _outproj(ya, yh, ys, w_out_l, x2, norm_w[l + 1], w_dt[l + 1])
        else:
            out = _outproj(ya, yh, ys, w_out_l, x2, final_norm_w)
    return out.reshape(batch, s, d)
```

```python
import functools

import numpy as np
import jax
import jax.numpy as jnp
from jax import lax
from jax.experimental import pallas as pl
from jax.experimental.pallas import tpu as pltpu

F32 = jnp.float32
BF16 = jnp.bfloat16

D_MODEL = 4096
DEPTH = 2
NORM_EPS = 1e-6
ATTN_HEADS = 8
ATTN_HEAD_DIM = 128
ATTN_WIDTH = ATTN_HEADS * ATTN_HEAD_DIM
ATTN_PATTERNS = ((128, 1), (512, 4), (2048, 16))
ATTN_REACH = max(w for w, _ in ATTN_PATTERNS)
ROPE_THETA = 10000.0
HGRN_HEADS = 8
HGRN_DIM = 128
HGRN_WIDTH = HGRN_HEADS * HGRN_DIM
SSM_HEADS = 32
SSM_HEAD_DIM = 64
SSM_WIDTH = SSM_HEADS * SSM_HEAD_DIM
SSM_GROUPS = 8
SSM_HEADS_PER_GROUP = SSM_HEADS // SSM_GROUPS
SSM_GROUP_WIDTH = SSM_WIDTH // SSM_GROUPS
SSM_STATE = 128
SSM_CONV = 4
MIX_WIDTH = ATTN_WIDTH + HGRN_WIDTH + SSM_WIDTH

COL_AQ, COL_AK, COL_AV, COL_AG = 0, 1024, 2048, 3072
COL_HQ, COL_HF, COL_HI, COL_HG = 4096, 5120, 6144, 7168
COL_SZ = 8192
COL_SX = 10240
COL_SB = COL_SX + SSM_WIDTH
COL_SC = COL_SB + SSM_GROUPS * SSM_STATE
COL_DT = COL_SC + SSM_GROUPS * SSM_STATE
IN_MAIN = COL_DT

LANES = 128
SUBLANES = 8
VMEM_LIMIT = 56 * 1024 * 1024

NEG = -1e30
LOG2E = 1.4426950408889634

NORM_ROWS = 512
PROJ_TM, PROJ_TN = 1024, 1024
OUT_TM, OUT_TN = 512, 1024
CAST_ROWS = 256
NORM_CHUNK = 32
ROPE_ROWS = 512
ATTN_TQ = 256
ATTN_HEADS_PER_STEP = 4
HGRN_ROWS = 512
HGRN_CHUNK = 128
HGRN_HEADS_PER_STEP = 4
SSD_ROWS = 512
SSD_CHUNK = 128
SSD_DT_ROWS = 1024


def _params(*sem):
    return pltpu.CompilerParams(dimension_semantics=sem, vmem_limit_bytes=VMEM_LIMIT)


def _rmsnorm_dt_kernel(x_ref, w_ref, wdt_ref, h_ref, dt_ref):
    x = x_ref[...]
    ms = jnp.mean(x * x, axis=-1, keepdims=True)
    h = ((x * lax.rsqrt(ms + NORM_EPS)) * w_ref[...]).astype(BF16)
    h_ref[...] = h
    dt_ref[...] = jnp.dot(h, wdt_ref[...], preferred_element_type=F32)


def _rmsnorm_dt(x2, w, w_dt):
    m, d = x2.shape
    rows = pl.BlockSpec((NORM_ROWS, d), lambda i: (i, 0))
    return pl.pallas_call(
        _rmsnorm_dt_kernel,
        grid=(m // NORM_ROWS,),
        in_specs=[rows, pl.BlockSpec((1, d), lambda i: (0, 0)),
                  pl.BlockSpec((d, LANES), lambda i: (0, 0))],
        out_specs=[rows, pl.BlockSpec((NORM_ROWS, LANES), lambda i: (i, 0))],
        out_shape=[jax.ShapeDtypeStruct((m, d), BF16), jax.ShapeDtypeStruct((m, LANES), F32)],
        compiler_params=_params("parallel"),
        name="rmsnorm_dt",
    )(x2, w.reshape(1, d), w_dt)


def _inproj_kernel(h_ref, w_ref, o_ref, wbf_ref):
    @pl.when(pl.program_id(1) == 0)
    def _():
        def cast_rows(r, carry):
            rows = pl.ds(pl.multiple_of(r * CAST_ROWS, CAST_ROWS), CAST_ROWS)
            wbf_ref[rows, :] = w_ref[0, rows, :].astype(BF16)
            return carry

        lax.fori_loop(0, wbf_ref.shape[0] // CAST_ROWS, cast_rows, 0)

    o_ref[...] = lax.dot_general(h_ref[...], wbf_ref[...], (((1,), (1,)), ((), ())),
                                 preferred_element_type=F32)


def _inproj(h, w_in_t, layer):
    m, d = h.shape
    return pl.pallas_call(
        _inproj_kernel,
        grid=(IN_MAIN // PROJ_TN, m // PROJ_TM),
        in_specs=[pl.BlockSpec((PROJ_TM, d), lambda j, i: (i, 0)),
                  pl.BlockSpec((1, PROJ_TN, d), lambda j, i: (layer, j, 0),
                               pipeline_mode=pl.Buffered(1))],
        out_specs=pl.BlockSpec((PROJ_TM, PROJ_TN), lambda j, i: (i, j)),
        out_shape=jax.ShapeDtypeStruct((m, IN_MAIN), F32),
        scratch_shapes=[pltpu.VMEM((PROJ_TN, d), BF16)],
        compiler_params=_params("parallel", "arbitrary"),
        name="inproj",
    )(h, w_in_t)


def _outproj_kernel(final, ya_ref, yh_ref, ys_ref, wa_ref, wh_ref, ws_ref, x_ref, nw_ref, *rest):
    if final:
        out_ref, slab_ref = rest
    else:
        wdt_ref, xo_ref, out_ref, dt_ref, slab_ref = rest
    j = pl.program_id(1)
    n_slabs, _, tn = slab_ref.shape
    acc = jnp.dot(ya_ref[...], wa_ref[...], preferred_element_type=F32)
    acc += jnp.dot(yh_ref[...], wh_ref[...], preferred_element_type=F32)
    acc += jnp.dot(ys_ref[...], ws_ref[...], preferred_element_type=F32)
    x_new = x_ref[...] + acc
    slab_ref[j] = x_new
    if not final:
        xo_ref[...] = x_new

    @pl.when(j == n_slabs - 1)
    def _():
        def norm_rows(r, carry):
            rows = pl.ds(pl.multiple_of(r * NORM_CHUNK, NORM_CHUNK), NORM_CHUNK)
            ss = jnp.sum(slab_ref[0, rows, :] * slab_ref[0, rows, :], axis=-1, keepdims=True)
            for s in range(1, n_slabs):
                ss = ss + jnp.sum(slab_ref[s, rows, :] * slab_ref[s, rows, :], axis=-1, keepdims=True)
            scale = lax.rsqrt(ss / (n_slabs * tn) + NORM_EPS)
            for s in range(n_slabs):
                cols = slice(s * tn, (s + 1) * tn)
                out_ref[rows, cols] = ((slab_ref[s, rows, :] * scale)
                                       * nw_ref[:, cols]).astype(out_ref.dtype)
            return carry

        lax.fori_loop(0, slab_ref.shape[1] // NORM_CHUNK, norm_rows, 0)
        if not final:
            dt_ref[...] = jnp.dot(out_ref[...], wdt_ref[...], preferred_element_type=F32)


def _outproj(ya, yh, ys, w_out, x2, norm_w, w_dt=None):
    m, d = x2.shape
    final = w_dt is None
    row = lambda i, j: (i, 0)
    col = lambda i, j: (0, j)
    tile = lambda i, j: (i, j)
    const = lambda i, j: (0, 0)
    assert HGRN_WIDTH == ATTN_WIDTH and SSM_WIDTH == ATTN_WIDTH + HGRN_WIDTH
    in_specs = [pl.BlockSpec((OUT_TM, ATTN_WIDTH), row),
                pl.BlockSpec((OUT_TM, HGRN_WIDTH), row),
                pl.BlockSpec((OUT_TM, SSM_WIDTH), row),
                pl.BlockSpec((ATTN_WIDTH, OUT_TN), col),
                pl.BlockSpec((HGRN_WIDTH, OUT_TN), lambda i, j: (1, j)),
                pl.BlockSpec((SSM_WIDTH, OUT_TN), lambda i, j: (1, j)),
                pl.BlockSpec((OUT_TM, OUT_TN), tile),
                pl.BlockSpec((1, d), const)]
    args = [ya, yh, ys, w_out, w_out, w_out, x2, norm_w.reshape(1, d)]
    full_rows = pl.BlockSpec((OUT_TM, d), row)
    if final:
        out_specs = full_rows
        out_shape = jax.ShapeDtypeStruct((m, d), F32)
    else:
        in_specs.append(pl.BlockSpec((d, LANES), const))
        args.append(w_dt)
        out_specs = [pl.BlockSpec((OUT_TM, OUT_TN), tile), full_rows,
                     pl.BlockSpec((OUT_TM, LANES), row)]
        out_shape = [jax.ShapeDtypeStruct((m, d), F32), jax.ShapeDtypeStruct((m, d), BF16),
                     jax.ShapeDtypeStruct((m, LANES), F32)]
    return pl.pallas_call(
        functools.partial(_outproj_kernel, final),
        grid=(m // OUT_TM, d // OUT_TN),
        in_specs=in_specs,
        out_specs=out_specs,
        out_shape=out_shape,
        scratch_shapes=[pltpu.VMEM((d // OUT_TN, OUT_TM, OUT_TN), F32)],
        compiler_params=_params("parallel", "arbitrary"),
        name="outproj_final" if final else "outproj",
    )(*args)


def _rope_table_kernel(pos_ref, invf_ref, cos_ref, sin_ref):
    ang = pos_ref[0] * invf_ref[...]
    lane = lax.broadcasted_iota(jnp.int32, ang.shape, 1)
    cos_ref[0] = jnp.cos(ang)
    s = jnp.sin(ang)
    sin_ref[0] = jnp.where(lane < ATTN_HEAD_DIM // 2, -s, s)


def _rope_tables(positions):
    b, s = positions.shape
    half = ATTN_HEAD_DIM // 2
    inv_freq = ROPE_THETA ** (-jnp.arange(half, dtype=F32) / half)
    invf = jnp.concatenate([inv_freq, inv_freq]).reshape(1, ATTN_HEAD_DIM)
    posb = jnp.broadcast_to(positions.astype(F32)[..., None], (b, s, ATTN_HEAD_DIM))
    spec = pl.BlockSpec((1, ROPE_ROWS, ATTN_HEAD_DIM), lambda bi, i: (bi, i, 0))
    return pl.pallas_call(
        _rope_table_kernel,
        grid=(b, s // ROPE_ROWS),
        in_specs=[spec, pl.BlockSpec((1, ATTN_HEAD_DIM), lambda bi, i: (0, 0))],
        out_specs=[spec, spec],
        out_shape=[jax.ShapeDtypeStruct((b, s, ATTN_HEAD_DIM), F32)] * 2,
        compiler_params=_params("parallel", "parallel"),
        name="rope_tables",
    )(posb, invf)


def _rope_kernel(pad_blocks, q_ref, k_ref, v_ref, cos_ref, sin_ref, qo_ref, ko_ref, vo_ref):
    i = pl.program_id(1)
    cos = cos_ref[0]
    sin = sin_ref[0]
    half = ATTN_HEAD_DIM // 2
    scale = ATTN_HEAD_DIM ** -0.5 * LOG2E

    def rope(t):
        return t * cos + pltpu.roll(t, half, axis=1) * sin

    for h in range(ATTN_HEADS):
        sl = slice(h * ATTN_HEAD_DIM, (h + 1) * ATTN_HEAD_DIM)
        qo_ref[0, :, sl] = (rope(q_ref[0, :, sl]) * scale).astype(BF16)

    @pl.when(i < pad_blocks)
    def _():
        ko_ref[...] = jnp.zeros_like(ko_ref)
        vo_ref[...] = jnp.zeros_like(vo_ref)

    @pl.when(i >= pad_blocks)
    def _():
        for h in range(ATTN_HEADS):
            sl = slice(h * ATTN_HEAD_DIM, (h + 1) * ATTN_HEAD_DIM)
            ko_ref[0, :, sl] = rope(k_ref[0, :, sl]).astype(BF16)
        vo_ref[0] = v_ref[0].astype(BF16)


def _rope_apply(proj3, cos, sin):
    b, s, _ = proj3.shape
    pad_blocks = ATTN_REACH // ROPE_ROWS
    nblk = s // ROPE_ROWS + pad_blocks
    src = lambda c: (lambda bi, i: (bi, jnp.maximum(i - pad_blocks, 0), c))
    slab = lambda c: pl.BlockSpec((1, ROPE_ROWS, ATTN_WIDTH), src(c))
    tab = pl.BlockSpec((1, ROPE_ROWS, ATTN_HEAD_DIM), src(0))
    return pl.pallas_call(
        functools.partial(_rope_kernel, pad_blocks),
        grid=(b, nblk),
        in_specs=[slab(COL_AQ // ATTN_WIDTH), slab(COL_AK // ATTN_WIDTH), slab(COL_AV // ATTN_WIDTH),
                  tab, tab],
        out_specs=[pl.BlockSpec((1, ROPE_ROWS, ATTN_WIDTH), src(0)),
                   pl.BlockSpec((1, ROPE_ROWS, ATTN_WIDTH), lambda bi, i: (bi, i, 0)),
                   pl.BlockSpec((1, ROPE_ROWS, ATTN_WIDTH), lambda bi, i: (bi, i, 0))],
        out_shape=[jax.ShapeDtypeStruct((b, s, ATTN_WIDTH), BF16),
                   jax.ShapeDtypeStruct((b, s + ATTN_REACH, ATTN_WIDTH), BF16),
                   jax.ShapeDtypeStruct((b, s + ATTN_REACH, ATTN_WIDTH), BF16)],
        compiler_params=_params("parallel", "arbitrary"),
        name="rope_apply",
    )(proj3, proj3, proj3, cos, sin)


def _attn_bias_table(tq):
    r = np.arange(tq)[:, None]
    j = np.arange(ATTN_REACH + tq)[None, :]
    delta = r + ATTN_REACH - j
    count = np.zeros(delta.shape, np.int64)
    for window, dilation in ATTN_PATTERNS:
        count += (delta >= 0) & (delta % dilation == 0) & (delta <= window)
    return np.where(count > 0, np.log2(np.maximum(count, 1)), NEG).astype(np.float32)


def _attn_kernel(q_ref, k_ref, v_ref, g_ref, bias_ref, o_ref):
    i = pl.program_id(2)
    tq = q_ref.shape[1]
    span = ATTN_REACH + tq
    start = pl.multiple_of(i * tq, tq)
    keys = pl.ds(start, span)
    col = lax.broadcasted_iota(jnp.int32, (1, span), 1)
    in_seq = col >= ATTN_REACH - i * tq
    for hh in range(ATTN_HEADS_PER_STEP):
        hs = slice(hh * ATTN_HEAD_DIM, (hh + 1) * ATTN_HEAD_DIM)
        q = q_ref[0, :, hs]
        s = lax.dot_general(q, k_ref[0, keys, hs], (((1,), (1,)), ((), ())),
                            preferred_element_type=F32)
        s = jnp.where(in_seq, s + bias_ref[...], NEG)
        m = jnp.max(s, axis=-1, keepdims=True)
        p = jnp.exp2(s - m)
        den = jnp.sum(p, axis=-1, keepdims=True)
        o = jnp.dot(p.astype(BF16), v_ref[0, keys, hs], preferred_element_type=F32) / den
        g = g_ref[0, :, hs]
        o_ref[0, :, hs] = (o * (g * jax.nn.sigmoid(g))).astype(o_ref.dtype)


def _attention(q_rot, k_pad, v_pad, proj3):
    b, s, _ = q_rot.shape
    tq = ATTN_TQ
    width = ATTN_HEADS_PER_STEP * ATTN_HEAD_DIM
    bias = jnp.asarray(_attn_bias_table(tq))
    gate0 = COL_AG // width
    full = pl.BlockSpec((1, s + ATTN_REACH, width), lambda bi, h, i: (bi, 0, h),
                        pipeline_mode=pl.Buffered(1))
    return pl.pallas_call(
        _attn_kernel,
        grid=(b, ATTN_HEADS // ATTN_HEADS_PER_STEP, s // tq),
        in_specs=[pl.BlockSpec((1, tq, width), lambda bi, h, i: (bi, i, h)),
                  full, full,
                  pl.BlockSpec((1, tq, width), lambda bi, h, i: (bi, i, gate0 + h)),
                  pl.BlockSpec(bias.shape, lambda bi, h, i: (0, 0))],
        out_specs=pl.BlockSpec((1, tq, width), lambda bi, h, i: (bi, i, h)),
        out_shape=jax.ShapeDtypeStruct((b, s, ATTN_WIDTH), BF16),
        compiler_params=_params("parallel", "parallel", "arbitrary"),
        name="attention",
    )(q_rot, k_pad, v_pad, proj3, bias)


def _split3(x):
    hi = x.astype(BF16)
    r1 = x - hi.astype(F32)
    mid = r1.astype(BF16)
    lo = (r1 - mid.astype(F32)).astype(BF16)
    return hi, mid, lo


def _cumsum_rows_mxu(x, tril):
    n = x.shape[1]
    r = jnp.dot(tril, jnp.concatenate(_split3(x), axis=1), preferred_element_type=F32)
    return (r[:, :n] + r[:, n:2 * n]) + r[:, 2 * n:]


def _log1p_exp_neg_abs(x):
    return jnp.log(1.0 + jnp.exp(-jnp.abs(x)))


def _hgrn_pair_level(c):
    t = lax.broadcasted_iota(jnp.int32, (c, c), 0)
    s = lax.broadcasted_iota(jnp.int32, (c, c), 1)
    x = t ^ s
    level = jnp.full((c, c), -1, jnp.int32)
    m = 1
    while m < c:
        level = level + (x >= m).astype(jnp.int32)
        m *= 2
    return jnp.where(t >= s, level, -2)


def _hgrn_gates(fr, lb, log_lb, log1m_lb, tril):
    log_sig = jnp.minimum(fr, 0.0) - _log1p_exp_neg_abs(fr)
    lower = log1m_lb + log_sig
    g = jnp.maximum(log_lb, lower) + _log1p_exp_neg_abs(log_lb - lower)
    kk = (1.0 - lb) * jax.nn.sigmoid(-fr)
    return kk, _cumsum_rows_mxu(g, tril)


def _hgrn_chunk(q, kk, b, v, st, pair_level):
    c = q.shape[0]
    nt = (((1,), (1,)), ((), ()))
    tn = (((0,), (0,)), ((), ()))
    b_last = b[c - 1:c, :]

    o = lax.dot_general((q * jnp.exp(b)).astype(BF16), st.astype(BF16), nt,
                        preferred_element_type=F32)

    n = q.shape[1]
    a = jnp.where(pair_level == -1,
                  lax.dot_general(q.astype(BF16), kk.astype(BF16), nt, preferred_element_type=F32), 0.0)
    b3 = b.reshape(c // SUBLANES, SUBLANES, n)
    sub3 = lax.broadcasted_iota(jnp.int32, b3.shape, 1)
    level, m = 0, 1
    while m < c:
        if m >= SUBLANES:
            mid = jnp.concatenate(
                [jnp.broadcast_to(b[blk * 2 * m + m - 1:blk * 2 * m + m, :], (2 * m, n))
                 for blk in range(c // (2 * m))], axis=0)
        else:
            mid3 = jnp.broadcast_to(b3[:, m - 1:m, :], b3.shape)
            for first in range(2 * m, SUBLANES, 2 * m):
                mid3 = jnp.where(sub3 >= first,
                                 jnp.broadcast_to(b3[:, first + m - 1:first + m, :], b3.shape), mid3)
            mid = mid3.reshape(c, n)
        e = jnp.exp(-jnp.abs(b - mid))
        part = lax.dot_general((q * e).astype(BF16), (kk * e).astype(BF16), nt,
                               preferred_element_type=F32)
        a = jnp.where(pair_level == level, part, a)
        level, m = level + 1, 2 * m
    o = o + jnp.dot(a.astype(BF16), v.astype(BF16), preferred_element_type=F32)

    kd = kk * jnp.exp(b_last - b)
    st_new = st * jnp.exp(b_last) + lax.dot_general(v.astype(BF16), kd.astype(BF16), tn,
                                                    preferred_element_type=F32)
    return o, st_new


def _hgrn_kernel(q_ref, f_ref, i_ref, g_ref, lb_ref, llb_ref, l1m_ref, nw_ref, o_ref,
                 st_ref, kk_ref, b_ref):
    @pl.when(pl.program_id(2) == 0)
    def _():
        st_ref[...] = jnp.zeros_like(st_ref)

    rows = q_ref.shape[1]
    ri = lax.broadcasted_iota(jnp.int32, (HGRN_CHUNK, HGRN_CHUNK), 0)
    ci_ = lax.broadcasted_iota(jnp.int32, (HGRN_CHUNK, HGRN_CHUNK), 1)
    tril = (ri >= ci_).astype(BF16)
    pair_level = _hgrn_pair_level(HGRN_CHUNK)

    def gates(ci, carry):
        sl = pl.ds(pl.multiple_of(ci * HGRN_CHUNK, HGRN_CHUNK), HGRN_CHUNK)
        kk, b = _hgrn_gates(f_ref[0, sl, :], lb_ref[...], llb_ref[...], l1m_ref[...], tril)
        kk_ref[sl, :] = kk
        b_ref[sl, :] = b
        return carry

    lax.fori_loop(0, rows // HGRN_CHUNK, gates, 0, unroll=True)

    def body(ci, carry):
        r0 = pl.multiple_of(ci * HGRN_CHUNK, HGRN_CHUNK)
        sl = pl.ds(r0, HGRN_CHUNK)
        for hh in range(HGRN_HEADS_PER_STEP):
            hs = slice(hh * HGRN_DIM, (hh + 1) * HGRN_DIM)
            o, st = _hgrn_chunk(q_ref[0, sl, hs], kk_ref[sl, hs], b_ref[sl, hs], i_ref[0, sl, hs],
                                st_ref[hh], pair_level)
            st_ref[hh] = st
            o = o * lax.rsqrt(jnp.mean(o * o, axis=-1, keepdims=True) + NORM_EPS) * nw_ref[:, hs]
            g = g_ref[0, sl, hs]
            o_ref[0, sl, hs] = (o * (g * jax.nn.sigmoid(g))).astype(o_ref.dtype)
        return carry

    lax.fori_loop(0, rows // HGRN_CHUNK, body, 0, unroll=True)


def _hgrn(proj3, lb, norm_w):
    b, s, _ = proj3.shape
    lb = lb.reshape(1, HGRN_WIDTH)
    log_lb = jnp.log(lb)
    log1m_lb = jnp.log1p(-lb)
    width = HGRN_HEADS_PER_STEP * HGRN_DIM
    blk = lambda c0: pl.BlockSpec((1, HGRN_ROWS, width), lambda bi, h, t: (bi, t, c0 // width + h))
    par = pl.BlockSpec((1, width), lambda bi, h, t: (0, h))
    return pl.pallas_call(
        _hgrn_kernel,
        grid=(b, HGRN_HEADS // HGRN_HEADS_PER_STEP, s // HGRN_ROWS),
        in_specs=[blk(COL_HQ), blk(COL_HF), blk(COL_HI), blk(COL_HG), par, par, par, par],
        out_specs=pl.BlockSpec((1, HGRN_ROWS, width), lambda bi, h, t: (bi, t, h)),
        out_shape=jax.ShapeDtypeStruct((b, s, HGRN_WIDTH), BF16),
        scratch_shapes=[pltpu.VMEM((HGRN_HEADS_PER_STEP, HGRN_DIM, HGRN_DIM), F32),
                        pltpu.VMEM((HGRN_ROWS, width), F32),
                        pltpu.VMEM((HGRN_ROWS, width), F32)],
        compiler_params=_params("parallel", "parallel", "arbitrary"),
        name="hgrn2",
    )(proj3, proj3, proj3, proj3, lb, log_lb, log1m_lb, norm_w.reshape(1, HGRN_WIDTH))


def _softplus(x):
    return jnp.maximum(x, 0.0) + _log1p_exp_neg_abs(x)


def _silu(x):
    return x * jax.nn.sigmoid(x)


def _cumsum_lanes(x):
    n = x.shape[1]
    lane = lax.broadcasted_iota(jnp.int32, x.shape, 1)
    k = 1
    while k < n:
        x = x + jnp.where(lane >= k, pltpu.roll(x, k, axis=1), 0.0)
        k *= 2
    return x


def _ssd_dt_kernel(dt_ref, bias_ref, alog_ref, acrow_ref, accol_ref, dtcol_ref):
    heads, lc = SSM_HEADS, SSD_CHUNK
    neg_a = jnp.exp(alog_ref[...])
    zeros = jnp.zeros((LANES - 3 * heads, lc), F32)
    for c in range(dt_ref.shape[1] // lc):
        sl = slice(c * lc, (c + 1) * lc)
        raw = dt_ref[0, sl, :].T[:heads]
        dt = _softplus(raw + bias_ref[...])
        ac = _cumsum_lanes(dt * (-neg_a))
        acrow_ref[0, :, sl] = ac
        for val, dst in ((ac, accol_ref), (dt, dtcol_ref)):
            pieces = [p.astype(F32) for p in _split3(val)]
            dst[0, sl, :] = jnp.concatenate(pieces + [zeros], axis=0).T.astype(BF16)


def _ssd_dt(dt3, dt_bias, a_log):
    b, s, _ = dt3.shape
    rows = SSD_DT_ROWS
    col = pl.BlockSpec((1, rows, LANES), lambda bi, t: (bi, t, 0))
    par = pl.BlockSpec((SSM_HEADS, 1), lambda bi, t: (0, 0))
    return pl.pallas_call(
        _ssd_dt_kernel,
        grid=(b, s // rows),
        in_specs=[col, par, par],
        out_specs=[pl.BlockSpec((1, SSM_HEADS, rows), lambda bi, t: (bi, 0, t)), col, col],
        out_shape=[jax.ShapeDtypeStruct((b, SSM_HEADS, s), F32),
                   jax.ShapeDtypeStruct((b, s, LANES), BF16),
                   jax.ShapeDtypeStruct((b, s, LANES), BF16)],
        compiler_params=_params("parallel", "parallel"),
        name="ssd_dt",
    )(dt3, dt_bias.reshape(SSM_HEADS, 1), a_log.reshape(SSM_HEADS, 1))


SEL_AC_WIDE = SSM_HEADS_PER_GROUP * LANES
SEL_AC_X = SEL_AC_WIDE + SSM_GROUP_WIDTH
SEL_WIDTH = SEL_AC_X + SSM_GROUP_WIDTH


def _ssd_select_matrices():
    sel = np.zeros((SSM_GROUPS, 2 * LANES, SEL_WIDTH), np.float32)
    for g in range(SSM_GROUPS):
        for e in range(SSM_HEADS_PER_GROUP):
            head = g * SSM_HEADS_PER_GROUP + e
            for piece in range(3):
                r = piece * SSM_HEADS + head
                sel[g, r, e * LANES:(e + 1) * LANES] = 1.0
                sel[g, r, SEL_AC_WIDE + e * SSM_HEAD_DIM:SEL_AC_WIDE + (e + 1) * SSM_HEAD_DIM] = 1.0
                sel[g, LANES + r, SEL_AC_X + e * SSM_HEAD_DIM:SEL_AC_X + (e + 1) * SSM_HEAD_DIM] = 1.0
    return sel


def _ssd_kernel(z_ref, x_ref, b_ref, c_ref, acrow_ref, accol_ref, dtcol_ref, sel_ref,
                cw_ref, cb_ref, dsk_ref, nw_ref, o_ref, st_ref, stage_ref, xbc_ref, acr_ref):
    e_heads, p_dim, lc = SSM_HEADS_PER_GROUP, SSM_HEAD_DIM, SSD_CHUNK
    rows = x_ref.shape[1]
    xw, sw = SSM_GROUP_WIDTH, SSM_STATE
    n_slabs = (xw + 2 * sw) // LANES

    @pl.when(pl.program_id(2) == 0)
    def _():
        st_ref[...] = jnp.zeros_like(st_ref)
        stage_ref[:, :SUBLANES, :] = jnp.zeros((n_slabs, SUBLANES, LANES), F32)

    for slab in range(xw // LANES):
        stage_ref[slab, SUBLANES:, :] = x_ref[0, :, slab * LANES:(slab + 1) * LANES]
    stage_ref[xw // LANES, SUBLANES:, :] = b_ref[0]
    stage_ref[xw // LANES + 1, SUBLANES:, :] = c_ref[0]
    for ci in range(rows // lc):
        acr_ref[ci] = acrow_ref[0, 0, :, ci * lc:(ci + 1) * lc]

    li = lax.broadcasted_iota(jnp.int32, (lc, lc), 0)
    si = lax.broadcasted_iota(jnp.int32, (lc, lc), 1)
    causal = li >= si
    lane_head = lax.broadcasted_iota(jnp.int32, (1, xw), 1) // p_dim
    nt = (((1,), (1,)), ((), ()))
    tn = (((0,), (0,)), ((), ()))
    cw = cw_ref[0]
    cbias = cb_ref[0]

    def body(ci, carry):
        r0 = pl.multiple_of(ci * lc, lc)
        sl = pl.ds(r0, lc)
        for slab in range(n_slabs):
            lanes = slice(slab * LANES, (slab + 1) * LANES)
            for parity in range(2):
                acc = cbias[:, lanes]
                for j in range(SSM_CONV):
                    first = r0 + (SUBLANES - (SSM_CONV - 1) + j + parity)
                    tap = stage_ref[slab, pl.ds(first, lc // 2, stride=2), :]
                    acc = acc + tap * cw[j:j + 1, lanes]
                xbc_ref[slab, pl.ds(r0 + parity, lc // 2, stride=2), :] = _silu(acc)
        xs = jnp.concatenate([xbc_ref[slab, sl, :] for slab in range(xw // LANES)], axis=1)
        bm = xbc_ref[xw // LANES, sl, :].astype(BF16)
        cm = xbc_ref[xw // LANES + 1, sl, :].astype(BF16)

        pieces = jnp.concatenate([accol_ref[0, sl, :], dtcol_ref[0, sl, :]], axis=1)
        bc = jnp.dot(pieces, sel_ref[0], preferred_element_type=F32)
        ac_x = bc[:, SEL_AC_WIDE:SEL_AC_X]
        xdt = xs * bc[:, SEL_AC_X:]
        ac_last = ac_x[lc - 1:lc, :]
        ac_r = acr_ref[ci]

        cb = lax.dot_general(cm, bm, nt, preferred_element_type=F32)
        decayed, x_heads = [], []
        for e in range(e_heads):
            dif = bc[:, e * LANES:(e + 1) * LANES] - ac_r[e:e + 1, :]
            decayed.append((cb * jnp.exp(jnp.where(causal, dif, NEG))).astype(BF16))
            x_heads.append(jnp.where(lane_head == e, xdt, 0.0).astype(BF16))
        y = jnp.dot(jnp.concatenate(decayed, axis=1), jnp.concatenate(x_heads, axis=0),
                    preferred_element_type=F32)
        st = st_ref[...]
        y = y + jnp.dot(cm, st.astype(BF16), preferred_element_type=F32) * jnp.exp(ac_x)
        xdec = (xdt * jnp.exp(ac_last - ac_x)).astype(BF16)
        st_ref[...] = st * jnp.exp(ac_last) + lax.dot_general(bm, xdec, tn,
                                                              preferred_element_type=F32)
        y = y + xs * dsk_ref[...]
        y = y * _silu(z_ref[0, sl, :])
        y = y * lax.rsqrt(jnp.mean(y * y, axis=-1, keepdims=True) + NORM_EPS) * nw_ref[...]
        o_ref[0, sl, :] = y.astype(o_ref.dtype)
        return carry

    lax.fori_loop(0, rows // lc, body, 0, unroll=True)
    stage_ref[:, :SUBLANES, :] = stage_ref[:, rows:rows + SUBLANES, :]


def _ssd(proj3, dt3, conv_w, conv_b, dt_bias, a_log, d_skip, norm_w):
    b, s, _ = proj3.shape
    g, e = SSM_GROUPS, SSM_HEADS_PER_GROUP
    rows = SSD_ROWS
    ac_row, ac_col, dt_col = _ssd_dt(dt3, dt_bias, a_log)
    ac_row = ac_row.reshape(b, g, e, s)
    sel = jnp.asarray(_ssd_select_matrices(), BF16)

    def per_group(p):
        px = p[:, :SSM_WIDTH].reshape(-1, g, SSM_GROUP_WIDTH)
        pb = p[:, SSM_WIDTH:SSM_WIDTH + g * SSM_STATE].reshape(-1, g, SSM_STATE)
        pc = p[:, SSM_WIDTH + g * SSM_STATE:].reshape(-1, g, SSM_STATE)
        return jnp.concatenate([px, pb, pc], axis=2).transpose(1, 0, 2)

    conv_width = SSM_GROUP_WIDTH + 2 * SSM_STATE
    dsk = jnp.repeat(d_skip, SSM_HEAD_DIM).reshape(1, SSM_WIDTH)

    def act(c0, width):
        return pl.BlockSpec((1, rows, width), lambda bi, gi, t: (bi, t, c0 // width + gi))

    col = pl.BlockSpec((1, rows, LANES), lambda bi, gi, t: (bi, t, 0))
    grp = lambda nrows, width: pl.BlockSpec((1, nrows, width), lambda bi, gi, t: (gi, 0, 0))
    par = pl.BlockSpec((1, SSM_GROUP_WIDTH), lambda bi, gi, t: (0, gi))
    return pl.pallas_call(
        _ssd_kernel,
        grid=(b, g, s // rows),
        in_specs=[act(COL_SZ, SSM_GROUP_WIDTH), act(COL_SX, SSM_GROUP_WIDTH),
                  act(COL_SB, SSM_STATE), act(COL_SC, SSM_STATE),
                  pl.BlockSpec((1, 1, e, rows), lambda bi, gi, t: (bi, gi, 0, t)),
                  col, col, grp(2 * LANES, SEL_WIDTH),
                  grp(SSM_CONV, conv_width), grp(1, conv_width), par, par],
        out_specs=pl.BlockSpec((1, rows, SSM_GROUP_WIDTH), lambda bi, gi, t: (bi, t, gi)),
        out_shape=jax.ShapeDtypeStruct((b, s, SSM_WIDTH), BF16),
        scratch_shapes=[pltpu.VMEM((SSM_STATE, SSM_GROUP_WIDTH), F32),
                        pltpu.VMEM((conv_width // LANES, SUBLANES + rows, LANES), F32),
                        pltpu.VMEM((conv_width // LANES, rows, LANES), F32),
                        pltpu.VMEM((rows // SSD_CHUNK, e, SSD_CHUNK), F32)],
        compiler_params=_params("parallel", "parallel", "arbitrary"),
        name="ssd",
    )(proj3, proj3, proj3, proj3, ac_row, ac_col, dt_col, sel,
      per_group(conv_w), per_group(conv_b.reshape(1, -1)), dsk, norm_w.reshape(1, SSM_WIDTH))


def _mixers(h, dt, batch, cos, sin, w_in_t, layer, conv_w, conv_b, dt_bias, a_log, d_skip,
            hgrn_norm_w, ssm_norm_w, lb):
    m = h.shape[0]
    s = m // batch
    proj3 = _inproj(h, w_in_t, layer).reshape(batch, s, IN_MAIN)
    dt3 = dt.reshape(batch, s, LANES)
    q_rot, k_pad, v_pad = _rope_apply(proj3, cos, sin)
    ya = _attention(q_rot, k_pad, v_pad, proj3)
    yh = _hgrn(proj3, lb, hgrn_norm_w)
    ys = _ssd(proj3, dt3, conv_w, conv_b, dt_bias, a_log, d_skip, ssm_norm_w)
    return ya.reshape(m, -1), yh.reshape(m, -1), ys.reshape(m, -1)


def kernel(x, positions, norm_w, w_in, conv_w, conv_b, dt_bias, a_log, d_skip, hgrn_norm_w,
           ssm_norm_w, w_out, hgrn_lb_logits, final_norm_w):
    batch, s, d = x.shape
    p = jax.nn.softmax(hgrn_lb_logits.astype(F32), axis=0)
    cs = jnp.cumsum(p, axis=0)
    lb_all = cs - cs[0:1]
    w_dt = jnp.pad(w_in[:, :, IN_MAIN:], ((0, 0), (0, 0), (0, LANES - SSM_HEADS))).astype(BF16)
    w_in_t = jnp.swapaxes(w_in, 1, 2)
    cos, sin = _rope_tables(positions)
    x2 = x.reshape(batch * s, d)
    h, dt = _rmsnorm_dt(x2, norm_w[0], w_dt[0])
    for l in range(DEPTH):
        ya, yh, ys = _mixers(h, dt, batch, cos, sin, w_in_t, l, conv_w[l], conv_b[l], dt_bias[l],
                             a_log[l], d_skip[l], hgrn_norm_w[l], ssm_norm_w[l], lb_all[l])
        w_out_l = w_out[l].astype(BF16)
        if l + 1 < DEPTH:
            x2, h, dt = _outproj(ya, yh, ys, w_out_l, x2, norm_w[l + 1], w_dt[l + 1])
        else:
            out = _outproj(ya, yh, ys, w_out_l, x2, final_norm_w)
    return out.reshape(batch, s, d)
```

```python
import functools

import numpy as np
import jax
import jax.numpy as jnp
from jax import lax
from jax.experimental import pallas as pl
from jax.experimental.pallas import tpu as pltpu

F32 = jnp.float32
BF16 = jnp.bfloat16

D_MODEL = 4096
DEPTH = 2
NORM_EPS = 1e-6
ATTN_HEADS = 8
ATTN_HEAD_DIM = 128
ATTN_WIDTH = ATTN_HEADS * ATTN_HEAD_DIM
ATTN_PATTERNS = ((128, 1), (512, 4), (2048, 16))
ATTN_REACH = max(w for w, _ in ATTN_PATTERNS)
ROPE_THETA = 10000.0
HGRN_HEADS = 8
HGRN_DIM = 128
HGRN_WIDTH = HGRN_HEADS * HGRN_DIM
SSM_HEADS = 32
SSM_HEAD_DIM = 64
SSM_WIDTH = SSM_HEADS * SSM_HEAD_DIM
SSM_GROUPS = 8
SSM_HEADS_PER_GROUP = SSM_HEADS // SSM_GROUPS
SSM_GROUP_WIDTH = SSM_WIDTH // SSM_GROUPS
SSM_STATE = 128
SSM_CONV = 4
MIX_WIDTH = ATTN_WIDTH + HGRN_WIDTH + SSM_WIDTH

COL_AQ, COL_AK, COL_AV, COL_AG = 0, 1024, 2048, 3072
COL_HQ, COL_HF, COL_HI, COL_HG = 4096, 5120, 6144, 7168
COL_SZ = 8192
COL_SX = 10240
COL_SB = COL_SX + SSM_WIDTH
COL_SC = COL_SB + SSM_GROUPS * SSM_STATE
COL_DT = COL_SC + SSM_GROUPS * SSM_STATE
IN_MAIN = COL_DT

LANES = 128
SUBLANES = 8
VMEM_LIMIT = 60 * 1024 * 1024

NEG = -1e30
LOG2E = 1.4426950408889634

NORM_ROWS = 512
PROJ_TM, PROJ_TN = 1024, 1024
MID_TM, MID_TN = 1024, 512
OUT_TM, OUT_TN = 512, 1024
CAST_ROWS = 256
NORM_CHUNK = 32
ROPE_ROWS = 512
ATTN_TQ = 256
ATTN_HEADS_PER_STEP = 4
HGRN_ROWS = 512
HGRN_CHUNK = 128
HGRN_HEADS_PER_STEP = 4
SSD_ROWS = 512
SSD_CHUNK = 128
SSD_DT_ROWS = 1024


def _silu(x):
    hx = 0.5 * x
    return hx + hx * jnp.tanh(hx)


def _params(*sem):
    return pltpu.CompilerParams(dimension_semantics=sem, vmem_limit_bytes=VMEM_LIMIT)


def _rmsnorm_dt_kernel(x_ref, w_ref, wdt_ref, h_ref, dt_ref):
    x = x_ref[...]
    ms = jnp.mean(x * x, axis=-1, keepdims=True)
    h = ((x * lax.rsqrt(ms + NORM_EPS)) * w_ref[...]).astype(BF16)
    h_ref[...] = h
    dt_ref[...] = jnp.dot(h, wdt_ref[...], preferred_element_type=F32)


def _rmsnorm_dt(x2, w, w_dt):
    m, d = x2.shape
    rows = pl.BlockSpec((NORM_ROWS, d), lambda i: (i, 0))
    return pl.pallas_call(
        _rmsnorm_dt_kernel,
        grid=(m // NORM_ROWS,),
        in_specs=[rows, pl.BlockSpec((1, d), lambda i: (0, 0)),
                  pl.BlockSpec((d, LANES), lambda i: (0, 0))],
        out_specs=[rows, pl.BlockSpec((NORM_ROWS, LANES), lambda i: (i, 0))],
        out_shape=[jax.ShapeDtypeStruct((m, d), BF16), jax.ShapeDtypeStruct((m, LANES), F32)],
        compiler_params=_params("parallel"),
        name="rmsnorm_dt",
    )(x2, w.reshape(1, d), w_dt)


def _inproj_kernel(scaled, h_ref, w_ref, *rest):
    if scaled:
        rs_ref, o_ref, wbf_ref = rest
    else:
        o_ref, wbf_ref = rest

    @pl.when(pl.program_id(1) == 0)
    def _():
        def cast_rows(r, carry):
            rows = pl.ds(pl.multiple_of(r * CAST_ROWS, CAST_ROWS), CAST_ROWS)
            wbf_ref[rows, :] = w_ref[0, rows, :].astype(BF16)
            return carry

        lax.fori_loop(0, wbf_ref.shape[0] // CAST_ROWS, cast_rows, 0)

    prod = lax.dot_general(h_ref[...], wbf_ref[...], (((1,), (1,)), ((), ())),
                           preferred_element_type=F32)
    if scaled:
        prod = prod * pltpu.repeat(rs_ref[...], prod.shape[1] // LANES, axis=1)
    o_ref[...] = prod


def _inproj(h, w_in_t, layer, row_scale=None):
    m, d = h.shape
    scaled = row_scale is not None
    in_specs = [pl.BlockSpec((PROJ_TM, d), lambda j, i: (i, 0)),
                pl.BlockSpec((1, PROJ_TN, d), lambda j, i: (layer, j, 0),
                             pipeline_mode=pl.Buffered(1))]
    args = [h, w_in_t]
    if scaled:
        in_specs.append(pl.BlockSpec((PROJ_TM, LANES), lambda j, i: (i, 0)))
        args.append(row_scale)
    return pl.pallas_call(
        functools.partial(_inproj_kernel, scaled),
        grid=(IN_MAIN // PROJ_TN, m // PROJ_TM),
        in_specs=in_specs,
        out_specs=pl.BlockSpec((PROJ_TM, PROJ_TN), lambda j, i: (i, j)),
        out_shape=jax.ShapeDtypeStruct((m, IN_MAIN), F32),
        scratch_shapes=[pltpu.VMEM((PROJ_TN, d), BF16)],
        compiler_params=_params("parallel", "arbitrary"),
        name="inproj",
    )(*args)


def _outproj_tile(ya_ref, yh_ref, ys_ref, wa_ref, wh_ref, ws_ref, x_ref):
    acc = jnp.dot(ya_ref[...], wa_ref[...], preferred_element_type=F32)
    acc += jnp.dot(yh_ref[...], wh_ref[...], preferred_element_type=F32)
    acc += jnp.dot(ys_ref[...], ws_ref[...], preferred_element_type=F32)
    return x_ref[...] + acc


def _outproj_mid_kernel(ya_ref, yh_ref, ys_ref, wa_ref, wh_ref, ws_ref, x_ref, nw_ref, wdt_ref,
                        xo_ref, hu_ref, rs_ref, dt_ref, ss_ref, dtacc_ref):
    j = pl.program_id(1)
    x_new = _outproj_tile(ya_ref, yh_ref, ys_ref, wa_ref, wh_ref, ws_ref, x_ref)
    xo_ref[...] = x_new
    hu = (x_new * nw_ref[...]).astype(BF16)
    hu_ref[...] = hu
    ss = jnp.sum(x_new * x_new, axis=-1, keepdims=True)
    dt_part = jnp.dot(hu, wdt_ref[...], preferred_element_type=F32)

    @pl.when(j == 0)
    def _():
        ss_ref[...] = jnp.broadcast_to(ss, ss_ref.shape)
        dtacc_ref[...] = dt_part

    @pl.when(j > 0)
    def _():
        ss_ref[...] += jnp.broadcast_to(ss, ss_ref.shape)
        dtacc_ref[...] += dt_part

    @pl.when(j == pl.num_programs(1) - 1)
    def _():
        r = lax.rsqrt(ss_ref[...] / D_MODEL + NORM_EPS)
        rs_ref[...] = r
        dt_ref[...] = dtacc_ref[...] * r


def _outproj_final_kernel(ya_ref, yh_ref, ys_ref, wa_ref, wh_ref, ws_ref, x_ref, nw_ref,
                          out_ref, slab_ref):
    j = pl.program_id(1)
    n_slabs, _, tn = slab_ref.shape
    slab_ref[j] = _outproj_tile(ya_ref, yh_ref, ys_ref, wa_ref, wh_ref, ws_ref, x_ref)

    @pl.when(j == n_slabs - 1)
    def _():
        def norm_rows(r, carry):
            rows = pl.ds(pl.multiple_of(r * NORM_CHUNK, NORM_CHUNK), NORM_CHUNK)
            ss = jnp.sum(slab_ref[0, rows, :] * slab_ref[0, rows, :], axis=-1, keepdims=True)
            for s in range(1, n_slabs):
                ss = ss + jnp.sum(slab_ref[s, rows, :] * slab_ref[s, rows, :], axis=-1, keepdims=True)
            scale = lax.rsqrt(ss / (n_slabs * tn) + NORM_EPS)
            for s in range(n_slabs):
                cols = slice(s * tn, (s + 1) * tn)
                out_ref[rows, cols] = (slab_ref[s, rows, :] * scale) * nw_ref[:, cols]
            return carry

        lax.fori_loop(0, slab_ref.shape[1] // NORM_CHUNK, norm_rows, 0, unroll=2)


def _outproj_specs(tm, tn):
    row = lambda i, j: (i, 0)
    assert HGRN_WIDTH == ATTN_WIDTH and SSM_WIDTH == ATTN_WIDTH + HGRN_WIDTH
    return [pl.BlockSpec((tm, ATTN_WIDTH), row),
            pl.BlockSpec((tm, HGRN_WIDTH), row),
            pl.BlockSpec((tm, SSM_WIDTH), row),
            pl.BlockSpec((ATTN_WIDTH, tn), lambda i, j: (0, j)),
            pl.BlockSpec((HGRN_WIDTH, tn), lambda i, j: (1, j)),
            pl.BlockSpec((SSM_WIDTH, tn), lambda i, j: (1, j)),
            pl.BlockSpec((tm, tn), lambda i, j: (i, j))]


def _outproj_mid(ya, yh, ys, w_out, x2, norm_w, w_dt):
    m, d = x2.shape
    tm, tn = MID_TM, MID_TN
    tile = pl.BlockSpec((tm, tn), lambda i, j: (i, j))
    stat = pl.BlockSpec((tm, LANES), lambda i, j: (i, 0))
    return pl.pallas_call(
        _outproj_mid_kernel,
        grid=(m // tm, d // tn),
        in_specs=_outproj_specs(tm, tn) + [pl.BlockSpec((1, tn), lambda i, j: (0, j)),
                                           pl.BlockSpec((tn, LANES), lambda i, j: (j, 0))],
        out_specs=[tile, tile, stat, stat],
        out_shape=[jax.ShapeDtypeStruct((m, d), F32), jax.ShapeDtypeStruct((m, d), BF16),
                   jax.ShapeDtypeStruct((m, LANES), F32), jax.ShapeDtypeStruct((m, LANES), F32)],
        scratch_shapes=[pltpu.VMEM((tm, LANES), F32), pltpu.VMEM((tm, LANES), F32)],
        compiler_params=_params("parallel", "arbitrary"),
        name="outproj",
    )(ya, yh, ys, w_out, w_out, w_out, x2, norm_w.reshape(1, d), w_dt)


def _outproj_final(ya, yh, ys, w_out, x2, norm_w):
    m, d = x2.shape
    return pl.pallas_call(
        _outproj_final_kernel,
        grid=(m // OUT_TM, d // OUT_TN),
        in_specs=_outproj_specs(OUT_TM, OUT_TN) + [pl.BlockSpec((1, d), lambda i, j: (0, 0))],
        out_specs=pl.BlockSpec((OUT_TM, d), lambda i, j: (i, 0)),
        out_shape=jax.ShapeDtypeStruct((m, d), F32),
        scratch_shapes=[pltpu.VMEM((d // OUT_TN, OUT_TM, OUT_TN), F32)],
        compiler_params=_params("parallel", "arbitrary"),
        name="outproj_final",
    )(ya, yh, ys, w_out, w_out, w_out, x2, norm_w.reshape(1, d))


def _rope_table_kernel(pos_ref, invf_ref, cos_ref, sin_ref):
    ang = pos_ref[0] * invf_ref[...]
    lane = lax.broadcasted_iota(jnp.int32, ang.shape, 1)
    cos_ref[0] = jnp.cos(ang)
    s = jnp.sin(ang)
    sin_ref[0] = jnp.where(lane < ATTN_HEAD_DIM // 2, -s, s)


def _rope_tables(positions):
    b, s = positions.shape
    half = ATTN_HEAD_DIM // 2
    inv_freq = ROPE_THETA ** (-jnp.arange(half, dtype=F32) / half)
    invf = jnp.concatenate([inv_freq, inv_freq]).reshape(1, ATTN_HEAD_DIM)
    posb = jnp.broadcast_to(positions.astype(F32)[..., None], (b, s, ATTN_HEAD_DIM))
    spec = pl.BlockSpec((1, ROPE_ROWS, ATTN_HEAD_DIM), lambda bi, i: (bi, i, 0))
    return pl.pallas_call(
        _rope_table_kernel,
        grid=(b, s // ROPE_ROWS),
        in_specs=[spec, pl.BlockSpec((1, ATTN_HEAD_DIM), lambda bi, i: (0, 0))],
        out_specs=[spec, spec],
        out_shape=[jax.ShapeDtypeStruct((b, s, ATTN_HEAD_DIM), F32)] * 2,
        compiler_params=_params("parallel", "parallel"),
        name="rope_tables",
    )(posb, invf)


def _rope_kernel(pad_blocks, q_ref, k_ref, v_ref, cos_ref, sin_ref, qo_ref, ko_ref, vo_ref):
    i = pl.program_id(1)
    cos = cos_ref[0]
    sin = sin_ref[0]
    half = ATTN_HEAD_DIM // 2
    scale = ATTN_HEAD_DIM ** -0.5 * LOG2E

    def rope(t):
        return t * cos + pltpu.roll(t, half, axis=1) * sin

    for h in range(ATTN_HEADS):
        sl = slice(h * ATTN_HEAD_DIM, (h + 1) * ATTN_HEAD_DIM)
        qo_ref[0, :, sl] = (rope(q_ref[0, :, sl]) * scale).astype(BF16)

    @pl.when(i < pad_blocks)
    def _():
        ko_ref[...] = jnp.zeros_like(ko_ref)
        vo_ref[...] = jnp.zeros_like(vo_ref)

    @pl.when(i >= pad_blocks)
    def _():
        for h in range(ATTN_HEADS):
            sl = slice(h * ATTN_HEAD_DIM, (h + 1) * ATTN_HEAD_DIM)
            ko_ref[0, :, sl] = rope(k_ref[0, :, sl]).astype(BF16)
        vo_ref[0] = v_ref[0].astype(BF16)


def _rope_apply(proj3, cos, sin):
    b, s, _ = proj3.shape
    pad_blocks = ATTN_REACH // ROPE_ROWS
    nblk = s // ROPE_ROWS + pad_blocks
    src = lambda c: (lambda bi, i: (bi, jnp.maximum(i - pad_blocks, 0), c))
    slab = lambda c: pl.BlockSpec((1, ROPE_ROWS, ATTN_WIDTH), src(c))
    tab = pl.BlockSpec((1, ROPE_ROWS, ATTN_HEAD_DIM), src(0))
    return pl.pallas_call(
        functools.partial(_rope_kernel, pad_blocks),
        grid=(b, nblk),
        in_specs=[slab(COL_AQ // ATTN_WIDTH), slab(COL_AK // ATTN_WIDTH), slab(COL_AV // ATTN_WIDTH),
                  tab, tab],
        out_specs=[pl.BlockSpec((1, ROPE_ROWS, ATTN_WIDTH), src(0)),
                   pl.BlockSpec((1, ROPE_ROWS, ATTN_WIDTH), lambda bi, i: (bi, i, 0)),
                   pl.BlockSpec((1, ROPE_ROWS, ATTN_WIDTH), lambda bi, i: (bi, i, 0))],
        out_shape=[jax.ShapeDtypeStruct((b, s, ATTN_WIDTH), BF16),
                   jax.ShapeDtypeStruct((b, s + ATTN_REACH, ATTN_WIDTH), BF16),
                   jax.ShapeDtypeStruct((b, s + ATTN_REACH, ATTN_WIDTH), BF16)],
        compiler_params=_params("parallel", "arbitrary"),
        name="rope_apply",
    )(proj3, proj3, proj3, cos, sin)


def _attn_bias_table(tq):
    r = np.arange(tq)[:, None]
    j = np.arange(ATTN_REACH + tq)[None, :]
    delta = r + ATTN_REACH - j
    count = np.zeros(delta.shape, np.int64)
    for window, dilation in ATTN_PATTERNS:
        count += (delta >= 0) & (delta % dilation == 0) & (delta <= window)
    return np.where(count > 0, np.log2(np.maximum(count, 1)), NEG).astype(np.float32)


def _attn_kernel(q_ref, k_ref, v_ref, g_ref, bias_ref, o_ref):
    i = pl.program_id(2)
    tq = q_ref.shape[1]
    span = ATTN_REACH + tq
    start = pl.multiple_of(i * tq, tq)
    keys = pl.ds(start, span)
    col = lax.broadcasted_iota(jnp.int32, (1, span), 1)
    in_seq = col >= ATTN_REACH - i * tq
    for hh in range(ATTN_HEADS_PER_STEP):
        hs = slice(hh * ATTN_HEAD_DIM, (hh + 1) * ATTN_HEAD_DIM)
        q = q_ref[0, :, hs]
        s = lax.dot_general(q, k_ref[0, keys, hs], (((1,), (1,)), ((), ())),
                            preferred_element_type=F32)
        s = jnp.where(in_seq, s + bias_ref[...], NEG)
        m = jnp.max(s, axis=-1, keepdims=True)
        p = jnp.exp2(s - m)
        den = jnp.sum(p, axis=-1, keepdims=True)
        o = jnp.dot(p.astype(BF16), v_ref[0, keys, hs], preferred_element_type=F32) / den
        g = g_ref[0, :, hs]
        o_ref[0, :, hs] = (o * _silu(g)).astype(o_ref.dtype)


def _attention(q_rot, k_pad, v_pad, proj3):
    b, s, _ = q_rot.shape
    tq = ATTN_TQ
    width = ATTN_HEADS_PER_STEP * ATTN_HEAD_DIM
    bias = jnp.asarray(_attn_bias_table(tq))
    gate0 = COL_AG // width
    full = pl.BlockSpec((1, s + ATTN_REACH, width), lambda bi, h, i: (bi, 0, h),
                        pipeline_mode=pl.Buffered(1))
    return pl.pallas_call(
        _attn_kernel,
        grid=(b, ATTN_HEADS // ATTN_HEADS_PER_STEP, s // tq),
        in_specs=[pl.BlockSpec((1, tq, width), lambda bi, h, i: (bi, i, h)),
                  full, full,
                  pl.BlockSpec((1, tq, width), lambda bi, h, i: (bi, i, gate0 + h)),
                  pl.BlockSpec(bias.shape, lambda bi, h, i: (0, 0))],
        out_specs=pl.BlockSpec((1, tq, width), lambda bi, h, i: (bi, i, h)),
        out_shape=jax.ShapeDtypeStruct((b, s, ATTN_WIDTH), BF16),
        compiler_params=_params("parallel", "parallel", "arbitrary"),
        name="attention",
    )(q_rot, k_pad, v_pad, proj3, bias)


def _split3(x):
    hi = x.astype(BF16)
    r1 = x - hi.astype(F32)
    mid = r1.astype(BF16)
    lo = (r1 - mid.astype(F32)).astype(BF16)
    return hi, mid, lo


def _cumsum_rows_mxu(x, tril):
    n = x.shape[1]
    r = jnp.dot(tril, jnp.concatenate(_split3(x), axis=1), preferred_element_type=F32)
    return (r[:, :n] + r[:, n:2 * n]) + r[:, 2 * n:]


def _log1p_exp_neg_abs(x):
    return jnp.log(1.0 + jnp.exp(-jnp.abs(x)))


def _hgrn_pair_level(c):
    t = lax.broadcasted_iota(jnp.int32, (c, c), 0)
    s = lax.broadcasted_iota(jnp.int32, (c, c), 1)
    x = t ^ s
    level = jnp.full((c, c), -1, jnp.int32)
    m = 1
    while m < c:
        level = level + (x >= m).astype(jnp.int32)
        m *= 2
    return jnp.where(t >= s, level, -2)


def _hgrn_gates(fr, lb, log_lb, log1m_lb, tril):
    log_sig = jnp.minimum(fr, 0.0) - _log1p_exp_neg_abs(fr)
    lower = log1m_lb + log_sig
    g = jnp.maximum(log_lb, lower) + _log1p_exp_neg_abs(log_lb - lower)
    kk = (1.0 - lb) * jax.nn.sigmoid(-fr)
    return kk, _cumsum_rows_mxu(g * LOG2E, tril)


def _hgrn_chunk(q, kk, b, v, st, pair_level):
    c = q.shape[0]
    nt = (((1,), (1,)), ((), ()))
    tn = (((0,), (0,)), ((), ()))
    b_last = b[c - 1:c, :]

    o = lax.dot_general((q * jnp.exp2(b)).astype(BF16), st.astype(BF16), nt,
                        preferred_element_type=F32)

    n = q.shape[1]
    a = jnp.where(pair_level == -1,
                  lax.dot_general(q.astype(BF16), kk.astype(BF16), nt, preferred_element_type=F32), 0.0)
    b3 = b.reshape(c // SUBLANES, SUBLANES, n)
    sub3 = lax.broadcasted_iota(jnp.int32, b3.shape, 1)
    level, m = 0, 1
    while m < c:
        if m >= SUBLANES:
            mid = jnp.concatenate(
                [jnp.broadcast_to(b[blk * 2 * m + m - 1:blk * 2 * m + m, :], (2 * m, n))
                 for blk in range(c // (2 * m))], axis=0)
        else:
            mid3 = jnp.broadcast_to(b3[:, m - 1:m, :], b3.shape)
            for first in range(2 * m, SUBLANES, 2 * m):
                mid3 = jnp.where(sub3 >= first,
                                 jnp.broadcast_to(b3[:, first + m - 1:first + m, :], b3.shape), mid3)
            mid = mid3.reshape(c, n)
        e = jnp.exp2(-jnp.abs(b - mid))
        part = lax.dot_general((q * e).astype(BF16), (kk * e).astype(BF16), nt,
                               preferred_element_type=F32)
        a = jnp.where(pair_level == level, part, a)
        level, m = level + 1, 2 * m
    o = o + jnp.dot(a.astype(BF16), v.astype(BF16), preferred_element_type=F32)

    kd = kk * jnp.exp2(b_last - b)
    st_new = st * jnp.exp2(b_last) + lax.dot_general(v.astype(BF16), kd.astype(BF16), tn,
                                                    preferred_element_type=F32)
    return o, st_new


def _hgrn_kernel(q_ref, f_ref, i_ref, g_ref, lb_ref, llb_ref, l1m_ref, nw_ref, o_ref,
                 st_ref, kk_ref, b_ref):
    @pl.when(pl.program_id(2) == 0)
    def _():
        st_ref[...] = jnp.zeros_like(st_ref)

    rows = q_ref.shape[1]
    ri = lax.broadcasted_iota(jnp.int32, (HGRN_CHUNK, HGRN_CHUNK), 0)
    ci_ = lax.broadcasted_iota(jnp.int32, (HGRN_CHUNK, HGRN_CHUNK), 1)
    tril = (ri >= ci_).astype(BF16)
    pair_level = _hgrn_pair_level(HGRN_CHUNK)

    def gates(ci, carry):
        sl = pl.ds(pl.multiple_of(ci * HGRN_CHUNK, HGRN_CHUNK), HGRN_CHUNK)
        kk, b = _hgrn_gates(f_ref[0, sl, :], lb_ref[...], llb_ref[...], l1m_ref[...], tril)
        kk_ref[sl, :] = kk
        b_ref[sl, :] = b
        return carry

    lax.fori_loop(0, rows // HGRN_CHUNK, gates, 0, unroll=True)

    def body(ci, carry):
        r0 = pl.multiple_of(ci * HGRN_CHUNK, HGRN_CHUNK)
        sl = pl.ds(r0, HGRN_CHUNK)
        for hh in range(HGRN_HEADS_PER_STEP):
            hs = slice(hh * HGRN_DIM, (hh + 1) * HGRN_DIM)
            o, st = _hgrn_chunk(q_ref[0, sl, hs], kk_ref[sl, hs], b_ref[sl, hs], i_ref[0, sl, hs],
                                st_ref[hh], pair_level)
            st_ref[hh] = st
            o = o * lax.rsqrt(jnp.mean(o * o, axis=-1, keepdims=True) + NORM_EPS) * nw_ref[:, hs]
            g = g_ref[0, sl, hs]
            o_ref[0, sl, hs] = (o * _silu(g)).astype(o_ref.dtype)
        return carry

    lax.fori_loop(0, rows // HGRN_CHUNK, body, 0, unroll=True)


def _hgrn(proj3, lb, norm_w):
    b, s, _ = proj3.shape
    lb = lb.reshape(1, HGRN_WIDTH)
    log_lb = jnp.log(lb)
    log1m_lb = jnp.log1p(-lb)
    width = HGRN_HEADS_PER_STEP * HGRN_DIM
    blk = lambda c0: pl.BlockSpec((1, HGRN_ROWS, width), lambda bi, h, t: (bi, t, c0 // width + h))
    par = pl.BlockSpec((1, width), lambda bi, h, t: (0, h))
    return pl.pallas_call(
        _hgrn_kernel,
        grid=(b, HGRN_HEADS // HGRN_HEADS_PER_STEP, s // HGRN_ROWS),
        in_specs=[blk(COL_HQ), blk(COL_HF), blk(COL_HI), blk(COL_HG), par, par, par, par],
        out_specs=pl.BlockSpec((1, HGRN_ROWS, width), lambda bi, h, t: (bi, t, h)),
        out_shape=jax.ShapeDtypeStruct((b, s, HGRN_WIDTH), BF16),
        scratch_shapes=[pltpu.VMEM((HGRN_HEADS_PER_STEP, HGRN_DIM, HGRN_DIM), F32),
                        pltpu.VMEM((HGRN_ROWS, width), F32),
                        pltpu.VMEM((HGRN_ROWS, width), F32)],
        compiler_params=_params("parallel", "parallel", "arbitrary"),
        name="hgrn2",
    )(proj3, proj3, proj3, proj3, lb, log_lb, log1m_lb, norm_w.reshape(1, HGRN_WIDTH))


def _softplus(x):
    return jnp.maximum(x, 0.0) + _log1p_exp_neg_abs(x)


def _cumsum_lanes(x):
    n = x.shape[1]
    lane = lax.broadcasted_iota(jnp.int32, x.shape, 1)
    k = 1
    while k < n:
        x = x + jnp.where(lane >= k, pltpu.roll(x, k, axis=1), 0.0)
        k *= 2
    return x


def _ssd_dt_kernel(dt_ref, bias_ref, alog_ref, acrow_ref, accol_ref, dtcol_ref):
    heads, lc = SSM_HEADS, SSD_CHUNK
    neg_a = jnp.exp(alog_ref[...])
    zeros = jnp.zeros((LANES - 3 * heads, lc), F32)
    for c in range(dt_ref.shape[1] // lc):
        sl = slice(c * lc, (c + 1) * lc)
        raw = dt_ref[0, sl, :].T[:heads]
        dt = _softplus(raw + bias_ref[...])
        ac = _cumsum_lanes(dt * (-LOG2E * neg_a))
        acrow_ref[0, :, sl] = ac
        for val, dst in ((ac, accol_ref), (dt, dtcol_ref)):
            pieces = [p.astype(F32) for p in _split3(val)]
            dst[0, sl, :] = jnp.concatenate(pieces + [zeros], axis=0).T.astype(BF16)


def _ssd_dt(dt3, dt_bias, a_log):
    b, s, _ = dt3.shape
    rows = SSD_DT_ROWS
    col = pl.BlockSpec((1, rows, LANES), lambda bi, t: (bi, t, 0))
    par = pl.BlockSpec((SSM_HEADS, 1), lambda bi, t: (0, 0))
    return pl.pallas_call(
        _ssd_dt_kernel,
        grid=(b, s // rows),
        in_specs=[col, par, par],
        out_specs=[pl.BlockSpec((1, SSM_HEADS, rows), lambda bi, t: (bi, 0, t)), col, col],
        out_shape=[jax.ShapeDtypeStruct((b, SSM_HEADS, s), F32),
                   jax.ShapeDtypeStruct((b, s, LANES), BF16),
                   jax.ShapeDtypeStruct((b, s, LANES), BF16)],
        compiler_params=_params("parallel", "parallel"),
        name="ssd_dt",
    )(dt3, dt_bias.reshape(SSM_HEADS, 1), a_log.reshape(SSM_HEADS, 1))


SEL_AC_WIDE = SSM_HEADS_PER_GROUP * LANES
SEL_AC_X = SEL_AC_WIDE + SSM_GROUP_WIDTH
SEL_WIDTH = SEL_AC_X + SSM_GROUP_WIDTH


def _ssd_select_matrices():
    sel = np.zeros((SSM_GROUPS, 2 * LANES, SEL_WIDTH), np.float32)
    for g in range(SSM_GROUPS):
        for e in range(SSM_HEADS_PER_GROUP):
            head = g * SSM_HEADS_PER_GROUP + e
            for piece in range(3):
                r = piece * SSM_HEADS + head
                sel[g, r, e * LANES:(e + 1) * LANES] = 1.0
                sel[g, r, SEL_AC_WIDE + e * SSM_HEAD_DIM:SEL_AC_WIDE + (e + 1) * SSM_HEAD_DIM] = 1.0
                sel[g, LANES + r, SEL_AC_X + e * SSM_HEAD_DIM:SEL_AC_X + (e + 1) * SSM_HEAD_DIM] = 1.0
    return sel


def _ssd_kernel(z_ref, x_ref, b_ref, c_ref, acrow_ref, accol_ref, dtcol_ref, sel_ref,
                cw_ref, cb_ref, dsk_ref, nw_ref, o_ref, st_ref, stage_ref, xbc_ref, acr_ref):
    e_heads, p_dim, lc = SSM_HEADS_PER_GROUP, SSM_HEAD_DIM, SSD_CHUNK
    rows = x_ref.shape[1]
    xw, sw = SSM_GROUP_WIDTH, SSM_STATE
    n_slabs = (xw + 2 * sw) // LANES

    @pl.when(pl.program_id(2) == 0)
    def _():
        st_ref[...] = jnp.zeros_like(st_ref)
        stage_ref[:, :SUBLANES, :] = jnp.zeros((n_slabs, SUBLANES, LANES), F32)

    for slab in range(xw // LANES):
        stage_ref[slab, SUBLANES:, :] = x_ref[0, :, slab * LANES:(slab + 1) * LANES]
    stage_ref[xw // LANES, SUBLANES:, :] = b_ref[0]
    stage_ref[xw // LANES + 1, SUBLANES:, :] = c_ref[0]
    for ci in range(rows // lc):
        acr_ref[ci] = acrow_ref[0, 0, :, ci * lc:(ci + 1) * lc]

    li = lax.broadcasted_iota(jnp.int32, (lc, lc), 0)
    si = lax.broadcasted_iota(jnp.int32, (lc, lc), 1)
    causal = li >= si
    lane_head = lax.broadcasted_iota(jnp.int32, (1, xw), 1) // p_dim
    nt = (((1,), (1,)), ((), ()))
    tn = (((0,), (0,)), ((), ()))
    cw = cw_ref[0]
    cbias = cb_ref[0]

    def body(ci, carry):
        r0 = pl.multiple_of(ci * lc, lc)
        sl = pl.ds(r0, lc)
        for slab in range(n_slabs):
            lanes = slice(slab * LANES, (slab + 1) * LANES)
            for parity in range(2):
                acc = cbias[:, lanes]
                for j in range(SSM_CONV):
                    first = r0 + (SUBLANES - (SSM_CONV - 1) + j + parity)
                    tap = stage_ref[slab, pl.ds(first, lc // 2, stride=2), :]
                    acc = acc + tap * cw[j:j + 1, lanes]
                xbc_ref[slab, pl.ds(r0 + parity, lc // 2, stride=2), :] = _silu(acc)
        xs = jnp.concatenate([xbc_ref[slab, sl, :] for slab in range(xw // LANES)], axis=1)
        bm = xbc_ref[xw // LANES, sl, :].astype(BF16)
        cm = xbc_ref[xw // LANES + 1, sl, :].astype(BF16)

        pieces = jnp.concatenate([accol_ref[0, sl, :], dtcol_ref[0, sl, :]], axis=1)
        bc = jnp.dot(pieces, sel_ref[0], preferred_element_type=F32)
        ac_x = bc[:, SEL_AC_WIDE:SEL_AC_X]
        xdt = xs * bc[:, SEL_AC_X:]
        ac_last = ac_x[lc - 1:lc, :]
        ac_r = acr_ref[ci]

        cb = lax.dot_general(cm, bm, nt, preferred_element_type=F32)
        decayed, x_heads = [], []
        for e in range(e_heads):
            dif = bc[:, e * LANES:(e + 1) * LANES] - ac_r[e:e + 1, :]
            decayed.append((cb * jnp.exp2(jnp.where(causal, dif, NEG))).astype(BF16))
            x_heads.append(jnp.where(lane_head == e, xdt, 0.0).astype(BF16))
        y = jnp.dot(jnp.concatenate(decayed, axis=1), jnp.concatenate(x_heads, axis=0),
                    preferred_element_type=F32)
        st = st_ref[...]
        y = y + jnp.dot(cm, st.astype(BF16), preferred_element_type=F32) * jnp.exp2(ac_x)
        xdec = (xdt * jnp.exp2(ac_last - ac_x)).astype(BF16)
        st_ref[...] = st * jnp.exp2(ac_last) + lax.dot_general(bm, xdec, tn,
                                                              preferred_element_type=F32)
        y = y + xs * dsk_ref[...]
        y = y * _silu(z_ref[0, sl, :])
        y = y * lax.rsqrt(jnp.mean(y * y, axis=-1, keepdims=True) + NORM_EPS) * nw_ref[...]
        o_ref[0, sl, :] = y.astype(o_ref.dtype)
        return carry

    lax.fori_loop(0, rows // lc, body, 0, unroll=True)
    stage_ref[:, :SUBLANES, :] = stage_ref[:, rows:rows + SUBLANES, :]


def _ssd(proj3, dt3, conv_w, conv_b, dt_bias, a_log, d_skip, norm_w):
    b, s, _ = proj3.shape
    g, e = SSM_GROUPS, SSM_HEADS_PER_GROUP
    rows = SSD_ROWS
    ac_row, ac_col, dt_col = _ssd_dt(dt3, dt_bias, a_log)
    ac_row = ac_row.reshape(b, g, e, s)
    sel = jnp.asarray(_ssd_select_matrices(), BF16)

    def per_group(p):
        px = p[:, :SSM_WIDTH].reshape(-1, g, SSM_GROUP_WIDTH)
        pb = p[:, SSM_WIDTH:SSM_WIDTH + g * SSM_STATE].reshape(-1, g, SSM_STATE)
        pc = p[:, SSM_WIDTH + g * SSM_STATE:].reshape(-1, g, SSM_STATE)
        return jnp.concatenate([px, pb, pc], axis=2).transpose(1, 0, 2)

    conv_width = SSM_GROUP_WIDTH + 2 * SSM_STATE
    dsk = jnp.repeat(d_skip, SSM_HEAD_DIM).reshape(1, SSM_WIDTH)

    def act(c0, width):
        return pl.BlockSpec((1, rows, width), lambda bi, gi, t: (bi, t, c0 // width + gi))

    col = pl.BlockSpec((1, rows, LANES), lambda bi, gi, t: (bi, t, 0))
    grp = lambda nrows, width: pl.BlockSpec((1, nrows, width), lambda bi, gi, t: (gi, 0, 0))
    par = pl.BlockSpec((1, SSM_GROUP_WIDTH), lambda bi, gi, t: (0, gi))
    return pl.pallas_call(
        _ssd_kernel,
        grid=(b, g, s // rows),
        in_specs=[act(COL_SZ, SSM_GROUP_WIDTH), act(COL_SX, SSM_GROUP_WIDTH),
                  act(COL_SB, SSM_STATE), act(COL_SC, SSM_STATE),
                  pl.BlockSpec((1, 1, e, rows), lambda bi, gi, t: (bi, gi, 0, t)),
                  col, col, grp(2 * LANES, SEL_WIDTH),
                  grp(SSM_CONV, conv_width), grp(1, conv_width), par, par],
        out_specs=pl.BlockSpec((1, rows, SSM_GROUP_WIDTH), lambda bi, gi, t: (bi, t, gi)),
        out_shape=jax.ShapeDtypeStruct((b, s, SSM_WIDTH), BF16),
        scratch_shapes=[pltpu.VMEM((SSM_STATE, SSM_GROUP_WIDTH), F32),
                        pltpu.VMEM((conv_width // LANES, SUBLANES + rows, LANES), F32),
                        pltpu.VMEM((conv_width // LANES, rows, LANES), F32),
                        pltpu.VMEM((rows // SSD_CHUNK, e, SSD_CHUNK), F32)],
        compiler_params=_params("parallel", "parallel", "arbitrary"),
        name="ssd",
    )(proj3, proj3, proj3, proj3, ac_row, ac_col, dt_col, sel,
      per_group(conv_w), per_group(conv_b.reshape(1, -1)), dsk, norm_w.reshape(1, SSM_WIDTH))


def _mixers(h, row_scale, dt, batch, cos, sin, w_in_t, layer, conv_w, conv_b, dt_bias, a_log, d_skip,
            hgrn_norm_w, ssm_norm_w, lb):
    m = h.shape[0]
    s = m // batch
    proj3 = _inproj(h, w_in_t, layer, row_scale).reshape(batch, s, IN_MAIN)
    dt3 = dt.reshape(batch, s, LANES)
    q_rot, k_pad, v_pad = _rope_apply(proj3, cos, sin)
    ya = _attention(q_rot, k_pad, v_pad, proj3)
    yh = _hgrn(proj3, lb, hgrn_norm_w)
    ys = _ssd(proj3, dt3, conv_w, conv_b, dt_bias, a_log, d_skip, ssm_norm_w)
    return ya.reshape(m, -1), yh.reshape(m, -1), ys.reshape(m, -1)


def kernel(x, positions, norm_w, w_in, conv_w, conv_b, dt_bias, a_log, d_skip, hgrn_norm_w,
           ssm_norm_w, w_out, hgrn_lb_logits, final_norm_w):
    batch, s, d = x.shape
    p = jax.nn.softmax(hgrn_lb_logits.astype(F32), axis=0)
    cs = jnp.cumsum(p, axis=0)
    lb_all = cs - cs[0:1]
    w_dt = jnp.pad(w_in[:, :, IN_MAIN:], ((0, 0), (0, 0), (0, LANES - SSM_HEADS))).astype(BF16)
    w_in_t = jnp.swapaxes(w_in, 1, 2)
    cos, sin = _rope_tables(positions)
    x2 = x.reshape(batch * s, d)
    h, dt = _rmsnorm_dt(x2, norm_w[0], w_dt[0])
    row_scale = None
    for l in range(DEPTH):
        ya, yh, ys = _mixers(h, row_scale, dt, batch, cos, sin, w_in_t, l, conv_w[l], conv_b[l],
                             dt_bias[l], a_log[l], d_skip[l], hgrn_norm_w[l], ssm_norm_w[l], lb_all[l])
        w_out_l = w_out[l].astype(BF16)
        if l + 1 < DEPTH:
            x2, h, row_scale, dt = _outproj_mid(ya, yh, ys, w_out_l, x2, norm_w[l + 1], w_dt[l + 1])
        else:
            out = _outproj_final(ya, yh, ys, w_out_l, x2, final_norm_w)
    return out.reshape(batch, s, d)
```

```python
import functools

import numpy as np
import jax
import jax.numpy as jnp
from jax import lax
from jax.experimental import pallas as pl
from jax.experimental.pallas import tpu as pltpu

F32 = jnp.float32
BF16 = jnp.bfloat16

D_MODEL = 4096
DEPTH = 2
NORM_EPS = 1e-6
ATTN_HEADS = 8
ATTN_HEAD_DIM = 128
ATTN_WIDTH = ATTN_HEADS * ATTN_HEAD_DIM
ATTN_PATTERNS = ((128, 1), (512, 4), (2048, 16))
ATTN_REACH = max(w for w, _ in ATTN_PATTERNS)
ROPE_THETA = 10000.0
HGRN_HEADS = 8
HGRN_DIM = 128
HGRN_WIDTH = HGRN_HEADS * HGRN_DIM
SSM_HEADS = 32
SSM_HEAD_DIM = 64
SSM_WIDTH = SSM_HEADS * SSM_HEAD_DIM
SSM_GROUPS = 8
SSM_HEADS_PER_GROUP = SSM_HEADS // SSM_GROUPS
SSM_GROUP_WIDTH = SSM_WIDTH // SSM_GROUPS
SSM_STATE = 128
SSM_CONV = 4
MIX_WIDTH = ATTN_WIDTH + HGRN_WIDTH + SSM_WIDTH

COL_AQ, COL_AK, COL_AV, COL_AG = 0, 1024, 2048, 3072
COL_HQ, COL_HF, COL_HI, COL_HG = 4096, 5120, 6144, 7168
COL_SZ = 8192
COL_SX = 10240
COL_SB = COL_SX + SSM_WIDTH
COL_SC = COL_SB + SSM_GROUPS * SSM_STATE
COL_DT = COL_SC + SSM_GROUPS * SSM_STATE
IN_MAIN = COL_DT

LANES = 128
SUBLANES = 8
VMEM_LIMIT = 60 * 1024 * 1024

NEG = -1e30
LOG2E = 1.4426950408889634

NORM_ROWS = 512
PROJ_TM, PROJ_TN = 1024, 1024
MID_TM, MID_TN = 1024, 512
OUT_TM, OUT_TN = 512, 1024
CAST_ROWS = 256
NORM_CHUNK = 32
ROPE_ROWS = 512
ATTN_TQ = 256
ATTN_HEADS_PER_STEP = 4
ATTN_FAR_TQ = 512
HGRN_ROWS = 512
HGRN_CHUNK = 128
HGRN_HEADS_PER_STEP = 4
SSD_ROWS = 512
SSD_CHUNK = 128
SSD_DT_ROWS = 1024


def _silu(x):
    hx = 0.5 * x
    return hx + hx * jnp.tanh(hx)


def _params(*sem):
    return pltpu.CompilerParams(dimension_semantics=sem, vmem_limit_bytes=VMEM_LIMIT)


def _rmsnorm_dt_kernel(x_ref, w_ref, wdt_ref, h_ref, dt_ref):
    x = x_ref[...]
    ms = jnp.mean(x * x, axis=-1, keepdims=True)
    h = ((x * lax.rsqrt(ms + NORM_EPS)) * w_ref[...]).astype(BF16)
    h_ref[...] = h
    dt_ref[...] = jnp.dot(h, wdt_ref[...], preferred_element_type=F32)


def _rmsnorm_dt(x2, w, w_dt):
    m, d = x2.shape
    rows = pl.BlockSpec((NORM_ROWS, d), lambda i: (i, 0))
    return pl.pallas_call(
        _rmsnorm_dt_kernel,
        grid=(m // NORM_ROWS,),
        in_specs=[rows, pl.BlockSpec((1, d), lambda i: (0, 0)),
                  pl.BlockSpec((d, LANES), lambda i: (0, 0))],
        out_specs=[rows, pl.BlockSpec((NORM_ROWS, LANES), lambda i: (i, 0))],
        out_shape=[jax.ShapeDtypeStruct((m, d), BF16), jax.ShapeDtypeStruct((m, LANES), F32)],
        compiler_params=_params("parallel"),
        name="rmsnorm_dt",
    )(x2, w.reshape(1, d), w_dt)


def _inproj_kernel(scaled, h_ref, w_ref, *rest):
    if scaled:
        rs_ref, o_ref, wbf_ref = rest
    else:
        o_ref, wbf_ref = rest

    @pl.when(pl.program_id(1) == 0)
    def _():
        def cast_rows(r, carry):
            rows = pl.ds(pl.multiple_of(r * CAST_ROWS, CAST_ROWS), CAST_ROWS)
            wbf_ref[rows, :] = w_ref[0, rows, :].astype(BF16)
            return carry

        lax.fori_loop(0, wbf_ref.shape[0] // CAST_ROWS, cast_rows, 0)

    prod = lax.dot_general(h_ref[...], wbf_ref[...], (((1,), (1,)), ((), ())),
                           preferred_element_type=F32)
    if scaled:
        prod = prod * jnp.concatenate([rs_ref[...]] * (prod.shape[1] // LANES), axis=1)
    o_ref[...] = prod


def _inproj(h, w_in_t, layer, row_scale=None):
    m, d = h.shape
    scaled = row_scale is not None
    in_specs = [pl.BlockSpec((PROJ_TM, d), lambda j, i: (i, 0)),
                pl.BlockSpec((1, PROJ_TN, d), lambda j, i: (layer, j, 0),
                             pipeline_mode=pl.Buffered(1))]
    args = [h, w_in_t]
    if scaled:
        in_specs.append(pl.BlockSpec((PROJ_TM, LANES), lambda j, i: (i, 0)))
        args.append(row_scale)
    return pl.pallas_call(
        functools.partial(_inproj_kernel, scaled),
        grid=(IN_MAIN // PROJ_TN, m // PROJ_TM),
        in_specs=in_specs,
        out_specs=pl.BlockSpec((PROJ_TM, PROJ_TN), lambda j, i: (i, j)),
        out_shape=jax.ShapeDtypeStruct((m, IN_MAIN), F32),
        scratch_shapes=[pltpu.VMEM((PROJ_TN, d), BF16)],
        compiler_params=_params("parallel", "arbitrary"),
        name="inproj",
    )(*args)


def _outproj_tile(ya_ref, yh_ref, ys_ref, wa_ref, wh_ref, ws_ref, x_ref):
    acc = jnp.dot(ya_ref[...], wa_ref[...], preferred_element_type=F32)
    acc += jnp.dot(yh_ref[...], wh_ref[...], preferred_element_type=F32)
    acc += jnp.dot(ys_ref[...], ws_ref[...], preferred_element_type=F32)
    return x_ref[...] + acc


def _outproj_mid_kernel(ya_ref, yh_ref, ys_ref, wa_ref, wh_ref, ws_ref, x_ref, nw_ref, wdt_ref,
                        xo_ref, hu_ref, rs_ref, dt_ref, ss_ref, dtacc_ref):
    j = pl.program_id(1)
    x_new = _outproj_tile(ya_ref, yh_ref, ys_ref, wa_ref, wh_ref, ws_ref, x_ref)
    xo_ref[...] = x_new
    hu = (x_new * nw_ref[...]).astype(BF16)
    hu_ref[...] = hu
    ss = jnp.sum(x_new * x_new, axis=-1, keepdims=True)
    dt_part = jnp.dot(hu, wdt_ref[...], preferred_element_type=F32)

    @pl.when(j == 0)
    def _():
        ss_ref[...] = jnp.broadcast_to(ss, ss_ref.shape)
        dtacc_ref[...] = dt_part

    @pl.when(j > 0)
    def _():
        ss_ref[...] += jnp.broadcast_to(ss, ss_ref.shape)
        dtacc_ref[...] += dt_part

    @pl.when(j == pl.num_programs(1) - 1)
    def _():
        r = lax.rsqrt(ss_ref[...] / D_MODEL + NORM_EPS)
        rs_ref[...] = r
        dt_ref[...] = dtacc_ref[...] * r


def _outproj_final_kernel(ya_ref, yh_ref, ys_ref, wa_ref, wh_ref, ws_ref, x_ref, nw_ref,
                          out_ref, slab_ref):
    j = pl.program_id(1)
    n_slabs, _, tn = slab_ref.shape
    slab_ref[j] = _outproj_tile(ya_ref, yh_ref, ys_ref, wa_ref, wh_ref, ws_ref, x_ref)

    @pl.when(j == n_slabs - 1)
    def _():
        def norm_rows(r, carry):
            rows = pl.ds(pl.multiple_of(r * NORM_CHUNK, NORM_CHUNK), NORM_CHUNK)
            ss = jnp.sum(slab_ref[0, rows, :] * slab_ref[0, rows, :], axis=-1, keepdims=True)
            for s in range(1, n_slabs):
                ss = ss + jnp.sum(slab_ref[s, rows, :] * slab_ref[s, rows, :], axis=-1, keepdims=True)
            scale = lax.rsqrt(ss / (n_slabs * tn) + NORM_EPS)
            for s in range(n_slabs):
                cols = slice(s * tn, (s + 1) * tn)
                out_ref[rows, cols] = (slab_ref[s, rows, :] * scale) * nw_ref[:, cols]
            return carry

        lax.fori_loop(0, slab_ref.shape[1] // NORM_CHUNK, norm_rows, 0, unroll=2)


def _outproj_specs(tm, tn):
    row = lambda i, j: (i, 0)
    assert HGRN_WIDTH == ATTN_WIDTH and SSM_WIDTH == ATTN_WIDTH + HGRN_WIDTH
    return [pl.BlockSpec((tm, ATTN_WIDTH), row),
            pl.BlockSpec((tm, HGRN_WIDTH), row),
            pl.BlockSpec((tm, SSM_WIDTH), row),
            pl.BlockSpec((ATTN_WIDTH, tn), lambda i, j: (0, j)),
            pl.BlockSpec((HGRN_WIDTH, tn), lambda i, j: (1, j)),
            pl.BlockSpec((SSM_WIDTH, tn), lambda i, j: (1, j)),
            pl.BlockSpec((tm, tn), lambda i, j: (i, j))]


def _outproj_mid(ya, yh, ys, w_out, x2, norm_w, w_dt):
    m, d = x2.shape
    tm, tn = MID_TM, MID_TN
    tile = pl.BlockSpec((tm, tn), lambda i, j: (i, j))
    stat = pl.BlockSpec((tm, LANES), lambda i, j: (i, 0))
    return pl.pallas_call(
        _outproj_mid_kernel,
        grid=(m // tm, d // tn),
        in_specs=_outproj_specs(tm, tn) + [pl.BlockSpec((1, tn), lambda i, j: (0, j)),
                                           pl.BlockSpec((tn, LANES), lambda i, j: (j, 0))],
        out_specs=[tile, tile, stat, stat],
        out_shape=[jax.ShapeDtypeStruct((m, d), F32), jax.ShapeDtypeStruct((m, d), BF16),
                   jax.ShapeDtypeStruct((m, LANES), F32), jax.ShapeDtypeStruct((m, LANES), F32)],
        scratch_shapes=[pltpu.VMEM((tm, LANES), F32), pltpu.VMEM((tm, LANES), F32)],
        compiler_params=_params("parallel", "arbitrary"),
        name="outproj",
    )(ya, yh, ys, w_out, w_out, w_out, x2, norm_w.reshape(1, d), w_dt)


def _outproj_final(ya, yh, ys, w_out, x2, norm_w):
    m, d = x2.shape
    return pl.pallas_call(
        _outproj_final_kernel,
        grid=(m // OUT_TM, d // OUT_TN),
        in_specs=_outproj_specs(OUT_TM, OUT_TN) + [pl.BlockSpec((1, d), lambda i, j: (0, 0))],
        out_specs=pl.BlockSpec((OUT_TM, d), lambda i, j: (i, 0)),
        out_shape=jax.ShapeDtypeStruct((m, d), F32),
        scratch_shapes=[pltpu.VMEM((d // OUT_TN, OUT_TM, OUT_TN), F32)],
        compiler_params=_params("parallel", "arbitrary"),
        name="outproj_final",
    )(ya, yh, ys, w_out, w_out, w_out, x2, norm_w.reshape(1, d))


def _rope_table_kernel(pos_ref, invf_ref, cos_ref, sin_ref):
    ang = pos_ref[0] * invf_ref[...]
    lane = lax.broadcasted_iota(jnp.int32, ang.shape, 1)
    cos_ref[0] = jnp.cos(ang)
    s = jnp.sin(ang)
    sin_ref[0] = jnp.where(lane < ATTN_HEAD_DIM // 2, -s, s)


def _rope_tables(positions):
    b, s = positions.shape
    half = ATTN_HEAD_DIM // 2
    inv_freq = ROPE_THETA ** (-jnp.arange(half, dtype=F32) / half)
    invf = jnp.concatenate([inv_freq, inv_freq]).reshape(1, ATTN_HEAD_DIM)
    posb = jnp.broadcast_to(positions.astype(F32)[..., None], (b, s, ATTN_HEAD_DIM))
    spec = pl.BlockSpec((1, ROPE_ROWS, ATTN_HEAD_DIM), lambda bi, i: (bi, i, 0))
    return pl.pallas_call(
        _rope_table_kernel,
        grid=(b, s // ROPE_ROWS),
        in_specs=[spec, pl.BlockSpec((1, ATTN_HEAD_DIM), lambda bi, i: (0, 0))],
        out_specs=[spec, spec],
        out_shape=[jax.ShapeDtypeStruct((b, s, ATTN_HEAD_DIM), F32)] * 2,
        compiler_params=_params("parallel", "parallel"),
        name="rope_tables",
    )(posb, invf)


def _rope_kernel(pad_blocks, q_ref, k_ref, v_ref, cos_ref, sin_ref, qo_ref, ko_ref, vo_ref):
    i = pl.program_id(1)
    cos = cos_ref[0]
    sin = sin_ref[0]
    half = ATTN_HEAD_DIM // 2
    scale = ATTN_HEAD_DIM ** -0.5 * LOG2E

    def rope(t):
        return t * cos + pltpu.roll(t, half, axis=1) * sin

    for h in range(ATTN_HEADS):
        sl = slice(h * ATTN_HEAD_DIM, (h + 1) * ATTN_HEAD_DIM)
        qo_ref[0, :, sl] = (rope(q_ref[0, :, sl]) * scale).astype(BF16)

    @pl.when(i < pad_blocks)
    def _():
        ko_ref[...] = jnp.zeros_like(ko_ref)
        vo_ref[...] = jnp.zeros_like(vo_ref)

    @pl.when(i >= pad_blocks)
    def _():
        for h in range(ATTN_HEADS):
            sl = slice(h * ATTN_HEAD_DIM, (h + 1) * ATTN_HEAD_DIM)
            ko_ref[0, :, sl] = rope(k_ref[0, :, sl]).astype(BF16)
        vo_ref[0] = v_ref[0].astype(BF16)


def _rope_apply(proj3, cos, sin):
    b, s, _ = proj3.shape
    pad_blocks = ATTN_REACH // ROPE_ROWS
    nblk = s // ROPE_ROWS + pad_blocks
    src = lambda c: (lambda bi, i: (bi, jnp.maximum(i - pad_blocks, 0), c))
    slab = lambda c: pl.BlockSpec((1, ROPE_ROWS, ATTN_WIDTH), src(c))
    tab = pl.BlockSpec((1, ROPE_ROWS, ATTN_HEAD_DIM), src(0))
    return pl.pallas_call(
        functools.partial(_rope_kernel, pad_blocks),
        grid=(b, nblk),
        in_specs=[slab(COL_AQ // ATTN_WIDTH), slab(COL_AK // ATTN_WIDTH), slab(COL_AV // ATTN_WIDTH),
                  tab, tab],
        out_specs=[pl.BlockSpec((1, ROPE_ROWS, ATTN_WIDTH), src(0)),
                   pl.BlockSpec((1, ROPE_ROWS, ATTN_WIDTH), lambda bi, i: (bi, i, 0)),
                   pl.BlockSpec((1, ROPE_ROWS, ATTN_WIDTH), lambda bi, i: (bi, i, 0))],
        out_shape=[jax.ShapeDtypeStruct((b, s, ATTN_WIDTH), BF16),
                   jax.ShapeDtypeStruct((b, s + ATTN_REACH, ATTN_WIDTH), BF16),
                   jax.ShapeDtypeStruct((b, s + ATTN_REACH, ATTN_WIDTH), BF16)],
        compiler_params=_params("parallel", "arbitrary"),
        name="rope_apply",
    )(proj3, proj3, proj3, cos, sin)


ATTN_FAR_WINDOW, ATTN_FAR_STRIDE = max(ATTN_PATTERNS)
ATTN_NEAR = sorted(w for w, _ in ATTN_PATTERNS)[-2]
assert ATTN_FAR_WINDOW == ATTN_REACH and ATTN_NEAR % ATTN_FAR_STRIDE == 0
assert (ATTN_REACH - ATTN_NEAR) % ATTN_TQ == 0


def _attn_near_table(tq):
    delta = np.arange(tq)[:, None] + ATTN_NEAR - np.arange(ATTN_NEAR + tq)[None, :]
    count = np.zeros(delta.shape, np.int64)
    for window, dilation in ATTN_PATTERNS:
        count += (delta >= 0) & (delta % dilation == 0) & (delta <= min(window, ATTN_NEAR))
    return np.where(count > 0, np.log2(np.maximum(count, 1)), NEG).astype(np.float32)


def _attn_far_table(tq):
    reach = ATTN_FAR_WINDOW // ATTN_FAR_STRIDE
    steps = np.arange(tq)[:, None] + reach - np.arange(reach + tq)[None, :]
    valid = (steps > ATTN_NEAR // ATTN_FAR_STRIDE) & (steps <= reach)
    return np.where(valid, 0.0, NEG).astype(np.float32)


def _attn_part(q, k, v, bias, in_seq):
    s = lax.dot_general(q, k, (((1,), (1,)), ((), ())), preferred_element_type=F32)
    s = jnp.where(in_seq, s + bias, NEG)
    m = jnp.max(s, axis=-1, keepdims=True)
    p = jnp.exp2(s - m)
    return m, jnp.sum(p, axis=-1, keepdims=True), jnp.dot(p.astype(BF16), v, preferred_element_type=F32)


def _attn_near_kernel(q_ref, k_ref, v_ref, bias_ref, o_ref, lse_ref):
    i = pl.program_id(2)
    tq = q_ref.shape[1]
    span = ATTN_NEAR + tq
    keys = pl.ds(pl.multiple_of(i * tq + (ATTN_REACH - ATTN_NEAR), tq), span)
    col = lax.broadcasted_iota(jnp.int32, (1, span), 1)
    in_seq = col >= ATTN_NEAR - i * tq
    for hh in range(ATTN_HEADS_PER_STEP):
        hs = slice(hh * ATTN_HEAD_DIM, (hh + 1) * ATTN_HEAD_DIM)
        m, l, acc = _attn_part(q_ref[0, :, hs], k_ref[0, keys, hs], v_ref[0, keys, hs],
                               bias_ref[...], in_seq)
        o_ref[0, :, hs] = acc / l
        lse_ref[0, :, hs] = jnp.broadcast_to(m + jnp.log2(l), acc.shape)


def _attn_far_kernel(q_ref, k_ref, v_ref, g_ref, bias_ref, on_ref, lse_ref, o_ref):
    i = pl.program_id(3)
    tq = q_ref.shape[1]
    reach = ATTN_FAR_WINDOW // ATTN_FAR_STRIDE
    span = reach + tq
    keys = pl.ds(pl.multiple_of(i * tq, tq), span)
    col = lax.broadcasted_iota(jnp.int32, (1, span), 1)
    in_seq = col >= reach - i * tq
    for hh in range(ATTN_HEADS_PER_STEP):
        hs = slice(hh * ATTN_HEAD_DIM, (hh + 1) * ATTN_HEAD_DIM)
        m_f, l_f, acc_f = _attn_part(q_ref[0, :, hs], k_ref[0, keys, hs], v_ref[0, keys, hs],
                                     bias_ref[...], in_seq)
        lse_n = lse_ref[0, :, hs]
        lse_f = m_f + jnp.log2(l_f)
        top = jnp.maximum(lse_n, lse_f)
        w_n, w_f = jnp.exp2(lse_n - top), jnp.exp2(lse_f - top)
        o = (w_n * on_ref[0, :, hs] + w_f * (acc_f / l_f)) / (w_n + w_f)
        o_ref[0, :, hs] = (o * _silu(g_ref[0, :, hs])).astype(o_ref.dtype)


def _attention(q_rot, k_pad, v_pad, proj3):
    b, s, _ = q_rot.shape
    tq = ATTN_TQ
    hps = ATTN_HEADS_PER_STEP
    width = hps * ATTN_HEAD_DIM
    groups = ATTN_HEADS // hps
    full = pl.BlockSpec((1, s + ATTN_REACH, width), lambda bi, h, i: (bi, 0, h),
                        pipeline_mode=pl.Buffered(1))
    tile = pl.BlockSpec((1, tq, width), lambda bi, h, i: (bi, i, h))
    near_bias = jnp.asarray(_attn_near_table(tq))
    o_near, lse_near = pl.pallas_call(
        _attn_near_kernel,
        grid=(b, groups, s // tq),
        in_specs=[tile, full, full, pl.BlockSpec(near_bias.shape, lambda bi, h, i: (0, 0))],
        out_specs=[tile, tile],
        out_shape=[jax.ShapeDtypeStruct((b, s, ATTN_WIDTH), F32)] * 2,
        compiler_params=_params("parallel", "parallel", "arbitrary"),
        name="attention_near",
    )(q_rot, k_pad, v_pad, near_bias)

    stride = ATTN_FAR_STRIDE
    rows = s // stride
    tq = min(ATTN_FAR_TQ, rows)
    by_class = lambda x: x.reshape(b, x.shape[1] // stride, stride * x.shape[2])
    blocks = lambda c: c // width
    cls_tile = lambda c, c0=0: pl.BlockSpec(
        (1, tq, width), lambda bi, h, r, i: (bi, i, r * blocks(c) + c0 // width + h))
    cls_full = pl.BlockSpec((1, (s + ATTN_REACH) // stride, width),
                            lambda bi, h, r, i: (bi, 0, r * groups + h), pipeline_mode=pl.Buffered(1))
    far_bias = jnp.asarray(_attn_far_table(tq))
    out = pl.pallas_call(
        _attn_far_kernel,
        grid=(b, groups, stride, rows // tq),
        in_specs=[cls_tile(ATTN_WIDTH), cls_full, cls_full, cls_tile(IN_MAIN, COL_AG),
                  pl.BlockSpec(far_bias.shape, lambda bi, h, r, i: (0, 0)),
                  cls_tile(ATTN_WIDTH), cls_tile(ATTN_WIDTH)],
        out_specs=cls_tile(ATTN_WIDTH),
        out_shape=jax.ShapeDtypeStruct((b, rows, stride * ATTN_WIDTH), BF16),
        compiler_params=_params("parallel", "parallel", "parallel", "arbitrary"),
        name="attention_far",
    )(by_class(q_rot), by_class(k_pad), by_class(v_pad), by_class(proj3), far_bias,
      by_class(o_near), by_class(lse_near))
    return out.reshape(b, s, ATTN_WIDTH)


def _split3(x):
    hi = x.astype(BF16)
    r1 = x - hi.astype(F32)
    mid = r1.astype(BF16)
    lo = (r1 - mid.astype(F32)).astype(BF16)
    return hi, mid, lo


def _cumsum_rows_mxu(x, tril):
    n = x.shape[1]
    r = jnp.dot(tril, jnp.concatenate(_split3(x), axis=1), preferred_element_type=F32)
    return (r[:, :n] + r[:, n:2 * n]) + r[:, 2 * n:]


def _log1p_exp_neg_abs(x):
    return jnp.log(1.0 + jnp.exp(-jnp.abs(x)))


def _hgrn_pair_level(c):
    t = lax.broadcasted_iota(jnp.int32, (c, c), 0)
    s = lax.broadcasted_iota(jnp.int32, (c, c), 1)
    x = t ^ s
    level = jnp.full((c, c), -1, jnp.int32)
    m = 1
    while m < c:
        level = level + (x >= m).astype(jnp.int32)
        m *= 2
    return jnp.where(t >= s, level, -2)


def _hgrn_gates(fr, lb, log_lb, log1m_lb, tril):
    log_sig = jnp.minimum(fr, 0.0) - _log1p_exp_neg_abs(fr)
    lower = log1m_lb + log_sig
    g = jnp.maximum(log_lb, lower) + _log1p_exp_neg_abs(log_lb - lower)
    kk = (1.0 - lb) * jax.nn.sigmoid(-fr)
    return kk, _cumsum_rows_mxu(g * LOG2E, tril)


def _hgrn_chunk(q, kk, b, v, st, pair_level):
    c = q.shape[0]
    nt = (((1,), (1,)), ((), ()))
    tn = (((0,), (0,)), ((), ()))
    b_last = b[c - 1:c, :]

    o = lax.dot_general((q * jnp.exp2(b)).astype(BF16), st.astype(BF16), nt,
                        preferred_element_type=F32)

    n = q.shape[1]
    a = jnp.where(pair_level == -1,
                  lax.dot_general(q.astype(BF16), kk.astype(BF16), nt, preferred_element_type=F32), 0.0)
    b3 = b.reshape(c // SUBLANES, SUBLANES, n)
    sub3 = lax.broadcasted_iota(jnp.int32, b3.shape, 1)
    level, m = 0, 1
    while m < c:
        if m >= SUBLANES:
            mid = jnp.concatenate(
                [jnp.broadcast_to(b[blk * 2 * m + m - 1:blk * 2 * m + m, :], (2 * m, n))
                 for blk in range(c // (2 * m))], axis=0)
        else:
            mid3 = jnp.broadcast_to(b3[:, m - 1:m, :], b3.shape)
            for first in range(2 * m, SUBLANES, 2 * m):
                mid3 = jnp.where(sub3 >= first,
                                 jnp.broadcast_to(b3[:, first + m - 1:first + m, :], b3.shape), mid3)
            mid = mid3.reshape(c, n)
        e = jnp.exp2(-jnp.abs(b - mid))
        part = lax.dot_general((q * e).astype(BF16), (kk * e).astype(BF16), nt,
                               preferred_element_type=F32)
        a = jnp.where(pair_level == level, part, a)
        level, m = level + 1, 2 * m
    o = o + jnp.dot(a.astype(BF16), v.astype(BF16), preferred_element_type=F32)

    kd = kk * jnp.exp2(b_last - b)
    st_new = st * jnp.exp2(b_last) + lax.dot_general(v.astype(BF16), kd.astype(BF16), tn,
                                                    preferred_element_type=F32)
    return o, st_new


def _hgrn_kernel(q_ref, f_ref, i_ref, g_ref, lb_ref, llb_ref, l1m_ref, nw_ref, o_ref,
                 st_ref, kk_ref, b_ref):
    @pl.when(pl.program_id(2) == 0)
    def _():
        st_ref[...] = jnp.zeros_like(st_ref)

    rows = q_ref.shape[1]
    ri = lax.broadcasted_iota(jnp.int32, (HGRN_CHUNK, HGRN_CHUNK), 0)
    ci_ = lax.broadcasted_iota(jnp.int32, (HGRN_CHUNK, HGRN_CHUNK), 1)
    tril = (ri >= ci_).astype(BF16)
    pair_level = _hgrn_pair_level(HGRN_CHUNK)

    def gates(ci, carry):
        sl = pl.ds(pl.multiple_of(ci * HGRN_CHUNK, HGRN_CHUNK), HGRN_CHUNK)
        kk, b = _hgrn_gates(f_ref[0, sl, :], lb_ref[...], llb_ref[...], l1m_ref[...], tril)
        kk_ref[sl, :] = kk
        b_ref[sl, :] = b
        return carry

    lax.fori_loop(0, rows // HGRN_CHUNK, gates, 0, unroll=True)

    def body(ci, carry):
        r0 = pl.multiple_of(ci * HGRN_CHUNK, HGRN_CHUNK)
        sl = pl.ds(r0, HGRN_CHUNK)
        for hh in range(HGRN_HEADS_PER_STEP):
            hs = slice(hh * HGRN_DIM, (hh + 1) * HGRN_DIM)
            o, st = _hgrn_chunk(q_ref[0, sl, hs], kk_ref[sl, hs], b_ref[sl, hs], i_ref[0, sl, hs],
                                st_ref[hh], pair_level)
            st_ref[hh] = st
            o = o * lax.rsqrt(jnp.mean(o * o, axis=-1, keepdims=True) + NORM_EPS) * nw_ref[:, hs]
            g = g_ref[0, sl, hs]
            o_ref[0, sl, hs] = (o * _silu(g)).astype(o_ref.dtype)
        return carry

    lax.fori_loop(0, rows // HGRN_CHUNK, body, 0, unroll=True)


def _hgrn(proj3, lb, norm_w):
    b, s, _ = proj3.shape
    lb = lb.reshape(1, HGRN_WIDTH)
    log_lb = jnp.log(lb)
    log1m_lb = jnp.log1p(-lb)
    width = HGRN_HEADS_PER_STEP * HGRN_DIM
    blk = lambda c0: pl.BlockSpec((1, HGRN_ROWS, width), lambda bi, h, t: (bi, t, c0 // width + h))
    par = pl.BlockSpec((1, width), lambda bi, h, t: (0, h))
    return pl.pallas_call(
        _hgrn_kernel,
        grid=(b, HGRN_HEADS // HGRN_HEADS_PER_STEP, s // HGRN_ROWS),
        in_specs=[blk(COL_HQ), blk(COL_HF), blk(COL_HI), blk(COL_HG), par, par, par, par],
        out_specs=pl.BlockSpec((1, HGRN_ROWS, width), lambda bi, h, t: (bi, t, h)),
        out_shape=jax.ShapeDtypeStruct((b, s, HGRN_WIDTH), BF16),
        scratch_shapes=[pltpu.VMEM((HGRN_HEADS_PER_STEP, HGRN_DIM, HGRN_DIM), F32),
                        pltpu.VMEM((HGRN_ROWS, width), F32),
                        pltpu.VMEM((HGRN_ROWS, width), F32)],
        compiler_params=_params("parallel", "parallel", "arbitrary"),
        name="hgrn2",
    )(proj3, proj3, proj3, proj3, lb, log_lb, log1m_lb, norm_w.reshape(1, HGRN_WIDTH))


def _softplus(x):
    return jnp.maximum(x, 0.0) + _log1p_exp_neg_abs(x)


def _cumsum_lanes(x):
    n = x.shape[1]
    lane = lax.broadcasted_iota(jnp.int32, x.shape, 1)
    k = 1
    while k < n:
        x = x + jnp.where(lane >= k, pltpu.roll(x, k, axis=1), 0.0)
        k *= 2
    return x


def _ssd_dt_kernel(dt_ref, bias_ref, alog_ref, acrow_ref, accol_ref, dtcol_ref):
    heads, lc = SSM_HEADS, SSD_CHUNK
    neg_a = jnp.exp(alog_ref[...])
    zeros = jnp.zeros((LANES - 3 * heads, lc), F32)
    for c in range(dt_ref.shape[1] // lc):
        sl = slice(c * lc, (c + 1) * lc)
        raw = dt_ref[0, sl, :].T[:heads]
        dt = _softplus(raw + bias_ref[...])
        ac = _cumsum_lanes(dt * (-LOG2E * neg_a))
        acrow_ref[0, :, sl] = ac
        for val, dst in ((ac, accol_ref), (dt, dtcol_ref)):
            pieces = [p.astype(F32) for p in _split3(val)]
            dst[0, sl, :] = jnp.concatenate(pieces + [zeros], axis=0).T.astype(BF16)


def _ssd_dt(dt3, dt_bias, a_log):
    b, s, _ = dt3.shape
    rows = SSD_DT_ROWS
    col = pl.BlockSpec((1, rows, LANES), lambda bi, t: (bi, t, 0))
    par = pl.BlockSpec((SSM_HEADS, 1), lambda bi, t: (0, 0))
    return pl.pallas_call(
        _ssd_dt_kernel,
        grid=(b, s // rows),
        in_specs=[col, par, par],
        out_specs=[pl.BlockSpec((1, SSM_HEADS, rows), lambda bi, t: (bi, 0, t)), col, col],
        out_shape=[jax.ShapeDtypeStruct((b, SSM_HEADS, s), F32),
                   jax.ShapeDtypeStruct((b, s, LANES), BF16),
                   jax.ShapeDtypeStruct((b, s, LANES), BF16)],
        compiler_params=_params("parallel", "parallel"),
        name="ssd_dt",
    )(dt3, dt_bias.reshape(SSM_HEADS, 1), a_log.reshape(SSM_HEADS, 1))


SEL_AC_WIDE = SSM_HEADS_PER_GROUP * LANES
SEL_AC_X = SEL_AC_WIDE + SSM_GROUP_WIDTH
SEL_WIDTH = SEL_AC_X + SSM_GROUP_WIDTH


def _ssd_select_matrices():
    sel = np.zeros((SSM_GROUPS, 2 * LANES, SEL_WIDTH), np.float32)
    for g in range(SSM_GROUPS):
        for e in range(SSM_HEADS_PER_GROUP):
            head = g * SSM_HEADS_PER_GROUP + e
            for piece in range(3):
                r = piece * SSM_HEADS + head
                sel[g, r, e * LANES:(e + 1) * LANES] = 1.0
                sel[g, r, SEL_AC_WIDE + e * SSM_HEAD_DIM:SEL_AC_WIDE + (e + 1) * SSM_HEAD_DIM] = 1.0
                sel[g, LANES + r, SEL_AC_X + e * SSM_HEAD_DIM:SEL_AC_X + (e + 1) * SSM_HEAD_DIM] = 1.0
    return sel


def _ssd_kernel(z_ref, x_ref, b_ref, c_ref, acrow_ref, accol_ref, dtcol_ref, sel_ref,
                cw_ref, cb_ref, dsk_ref, nw_ref, o_ref, st_ref, stage_ref, xbc_ref, acr_ref):
    e_heads, p_dim, lc = SSM_HEADS_PER_GROUP, SSM_HEAD_DIM, SSD_CHUNK
    rows = x_ref.shape[1]
    xw, sw = SSM_GROUP_WIDTH, SSM_STATE
    n_slabs = (xw + 2 * sw) // LANES

    @pl.when(pl.program_id(2) == 0)
    def _():
        st_ref[...] = jnp.zeros_like(st_ref)
        stage_ref[:, :SUBLANES, :] = jnp.zeros((n_slabs, SUBLANES, LANES), F32)

    for slab in range(xw // LANES):
        stage_ref[slab, SUBLANES:, :] = x_ref[0, :, slab * LANES:(slab + 1) * LANES]
    stage_ref[xw // LANES, SUBLANES:, :] = b_ref[0]
    stage_ref[xw // LANES + 1, SUBLANES:, :] = c_ref[0]
    for ci in range(rows // lc):
        acr_ref[ci] = acrow_ref[0, 0, :, ci * lc:(ci + 1) * lc]

    li = lax.broadcasted_iota(jnp.int32, (lc, lc), 0)
    si = lax.broadcasted_iota(jnp.int32, (lc, lc), 1)
    causal = li >= si
    lane_head = lax.broadcasted_iota(jnp.int32, (1, xw), 1) // p_dim
    nt = (((1,), (1,)), ((), ()))
    tn = (((0,), (0,)), ((), ()))
    cw = cw_ref[0]
    cbias = cb_ref[0]

    def body(ci, carry):
        r0 = pl.multiple_of(ci * lc, lc)
        sl = pl.ds(r0, lc)
        for slab in range(n_slabs):
            lanes = slice(slab * LANES, (slab + 1) * LANES)
            for parity in range(2):
                acc = cbias[:, lanes]
                for j in range(SSM_CONV):
                    first = r0 + (SUBLANES - (SSM_CONV - 1) + j + parity)
                    tap = stage_ref[slab, pl.ds(first, lc // 2, stride=2), :]
                    acc = acc + tap * cw[j:j + 1, lanes]
                xbc_ref[slab, pl.ds(r0 + parity, lc // 2, stride=2), :] = _silu(acc)
        xs = jnp.concatenate([xbc_ref[slab, sl, :] for slab in range(xw // LANES)], axis=1)
        bm = xbc_ref[xw // LANES, sl, :].astype(BF16)
        cm = xbc_ref[xw // LANES + 1, sl, :].astype(BF16)

        pieces = jnp.concatenate([accol_ref[0, sl, :], dtcol_ref[0, sl, :]], axis=1)
        bc = jnp.dot(pieces, sel_ref[0], preferred_element_type=F32)
        ac_x = bc[:, SEL_AC_WIDE:SEL_AC_X]
        xdt = xs * bc[:, SEL_AC_X:]
        ac_last = ac_x[lc - 1:lc, :]
        ac_r = acr_ref[ci]

        cb = lax.dot_general(cm, bm, nt, preferred_element_type=F32)
        decayed, x_heads = [], []
        for e in range(e_heads):
            dif = bc[:, e * LANES:(e + 1) * LANES] - ac_r[e:e + 1, :]
            decayed.append((cb * jnp.exp2(jnp.where(causal, dif, NEG))).astype(BF16))
            x_heads.append(jnp.where(lane_head == e, xdt, 0.0).astype(BF16))
        y = jnp.dot(jnp.concatenate(decayed, axis=1), jnp.concatenate(x_heads, axis=0),
                    preferred_element_type=F32)
        st = st_ref[...]
        y = y + jnp.dot(cm, st.astype(BF16), preferred_element_type=F32) * jnp.exp2(ac_x)
        xdec = (xdt * jnp.exp2(ac_last - ac_x)).astype(BF16)
        st_ref[...] = st * jnp.exp2(ac_last) + lax.dot_general(bm, xdec, tn,
                                                              preferred_element_type=F32)
        y = y + xs * dsk_ref[...]
        y = y * _silu(z_ref[0, sl, :])
        y = y * lax.rsqrt(jnp.mean(y * y, axis=-1, keepdims=True) + NORM_EPS) * nw_ref[...]
        o_ref[0, sl, :] = y.astype(o_ref.dtype)
        return carry

    lax.fori_loop(0, rows // lc, body, 0, unroll=True)
    stage_ref[:, :SUBLANES, :] = stage_ref[:, rows:rows + SUBLANES, :]


def _ssd(proj3, dt3, conv_w, conv_b, dt_bias, a_log, d_skip, norm_w):
    b, s, _ = proj3.shape
    g, e = SSM_GROUPS, SSM_HEADS_PER_GROUP
    rows = SSD_ROWS
    ac_row, ac_col, dt_col = _ssd_dt(dt3, dt_bias, a_log)
    ac_row = ac_row.reshape(b, g, e, s)
    sel = jnp.asarray(_ssd_select_matrices(), BF16)

    def per_group(p):
        px = p[:, :SSM_WIDTH].reshape(-1, g, SSM_GROUP_WIDTH)
        pb = p[:, SSM_WIDTH:SSM_WIDTH + g * SSM_STATE].reshape(-1, g, SSM_STATE)
        pc = p[:, SSM_WIDTH + g * SSM_STATE:].reshape(-1, g, SSM_STATE)
        return jnp.concatenate([px, pb, pc], axis=2).transpose(1, 0, 2)

    conv_width = SSM_GROUP_WIDTH + 2 * SSM_STATE
    dsk = jnp.repeat(d_skip, SSM_HEAD_DIM).reshape(1, SSM_WIDTH)

    def act(c0, width):
        return pl.BlockSpec((1, rows, width), lambda bi, gi, t: (bi, t, c0 // width + gi))

    col = pl.BlockSpec((1, rows, LANES), lambda bi, gi, t: (bi, t, 0))
    grp = lambda nrows, width: pl.BlockSpec((1, nrows, width), lambda bi, gi, t: (gi, 0, 0))
    par = pl.BlockSpec((1, SSM_GROUP_WIDTH), lambda bi, gi, t: (0, gi))
    return pl.pallas_call(
        _ssd_kernel,
        grid=(b, g, s // rows),
        in_specs=[act(COL_SZ, SSM_GROUP_WIDTH), act(COL_SX, SSM_GROUP_WIDTH),
                  act(COL_SB, SSM_STATE), act(COL_SC, SSM_STATE),
                  pl.BlockSpec((1, 1, e, rows), lambda bi, gi, t: (bi, gi, 0, t)),
                  col, col, grp(2 * LANES, SEL_WIDTH),
                  grp(SSM_CONV, conv_width), grp(1, conv_width), par, par],
        out_specs=pl.BlockSpec((1, rows, SSM_GROUP_WIDTH), lambda bi, gi, t: (bi, t, gi)),
        out_shape=jax.ShapeDtypeStruct((b, s, SSM_WIDTH), BF16),
        scratch_shapes=[pltpu.VMEM((SSM_STATE, SSM_GROUP_WIDTH), F32),
                        pltpu.VMEM((conv_width // LANES, SUBLANES + rows, LANES), F32),
                        pltpu.VMEM((conv_width // LANES, rows, LANES), F32),
                        pltpu.VMEM((rows // SSD_CHUNK, e, SSD_CHUNK), F32)],
        compiler_params=_params("parallel", "parallel", "arbitrary"),
        name="ssd",
    )(proj3, proj3, proj3, proj3, ac_row, ac_col, dt_col, sel,
      per_group(conv_w), per_group(conv_b.reshape(1, -1)), dsk, norm_w.reshape(1, SSM_WIDTH))


def _mixers(h, row_scale, dt, batch, cos, sin, w_in_t, layer, conv_w, conv_b, dt_bias, a_log, d_skip,
            hgrn_norm_w, ssm_norm_w, lb):
    m = h.shape[0]
    s = m // batch
    proj3 = _inproj(h, w_in_t, layer, row_scale).reshape(batch, s, IN_MAIN)
    dt3 = dt.reshape(batch, s, LANES)
    q_rot, k_pad, v_pad = _rope_apply(proj3, cos, sin)
    ya = _attention(q_rot, k_pad, v_pad, proj3)
    yh = _hgrn(proj3, lb, hgrn_norm_w)
    ys = _ssd(proj3, dt3, conv_w, conv_b, dt_bias, a_log, d_skip, ssm_norm_w)
    return ya.reshape(m, -1), yh.reshape(m, -1), ys.reshape(m, -1)


def kernel(x, positions, norm_w, w_in, conv_w, conv_b, dt_bias, a_log, d_skip, hgrn_norm_w,
           ssm_norm_w, w_out, hgrn_lb_logits, final_norm_w):
    batch, s, d = x.shape
    p = jax.nn.softmax(hgrn_lb_logits.astype(F32), axis=0)
    cs = jnp.cumsum(p, axis=0)
    lb_all = cs - cs[0:1]
    w_dt = jnp.pad(w_in[:, :, IN_MAIN:], ((0, 0), (0, 0), (0, LANES - SSM_HEADS))).astype(BF16)
    w_in_t = jnp.swapaxes(w_in, 1, 2)
    cos, sin = _rope_tables(positions)
    x2 = x.reshape(batch * s, d)
    h, dt = _rmsnorm_dt(x2, norm_w[0], w_dt[0])
    row_scale = None
    for l in range(DEPTH):
        ya, yh, ys = _mixers(h, row_scale, dt, batch, cos, sin, w_in_t, l, conv_w[l], conv_b[l],
                             dt_bias[l], a_log[l], d_skip[l], hgrn_norm_w[l], ssm_norm_w[l], lb_all[l])
        w_out_l = w_out[l].astype(BF16)
        if l + 1 < DEPTH:
            x2, h, row_scale, dt = _outproj_mid(ya, yh, ys, w_out_l, x2, norm_w[l + 1], w_dt[l + 1])
        else:
            out = _outproj_final(ya, yh, ys, w_out_l, x2, final_norm_w)
    return out.reshape(batch, s, d)
```

```python
import functools

import numpy as np
import jax
import jax.numpy as jnp
from jax import lax
from jax.experimental import pallas as pl
from jax.experimental.pallas import tpu as pltpu

F32 = jnp.float32
BF16 = jnp.bfloat16

D_MODEL = 4096
DEPTH = 2
NORM_EPS = 1e-6
ATTN_HEADS = 8
ATTN_HEAD_DIM = 128
ATTN_WIDTH = ATTN_HEADS * ATTN_HEAD_DIM
ATTN_PATTERNS = ((128, 1), (512, 4), (2048, 16))
ATTN_REACH = max(w for w, _ in ATTN_PATTERNS)
ROPE_THETA = 10000.0
HGRN_HEADS = 8
HGRN_DIM = 128
HGRN_WIDTH = HGRN_HEADS * HGRN_DIM
SSM_HEADS = 32
SSM_HEAD_DIM = 64
SSM_WIDTH = SSM_HEADS * SSM_HEAD_DIM
SSM_GROUPS = 8
SSM_HEADS_PER_GROUP = SSM_HEADS // SSM_GROUPS
SSM_GROUP_WIDTH = SSM_WIDTH // SSM_GROUPS
SSM_STATE = 128
SSM_CONV = 4
MIX_WIDTH = ATTN_WIDTH + HGRN_WIDTH + SSM_WIDTH

COL_AQ, COL_AK, COL_AV, COL_AG = 0, 1024, 2048, 3072
COL_HQ, COL_HF, COL_HI, COL_HG = 4096, 5120, 6144, 7168
COL_SZ = 8192
COL_SX = 10240
COL_SB = COL_SX + SSM_WIDTH
COL_SC = COL_SB + SSM_GROUPS * SSM_STATE
COL_DT = COL_SC + SSM_GROUPS * SSM_STATE
IN_MAIN = COL_DT

LANES = 128
SUBLANES = 8
VMEM_LIMIT = 60 * 1024 * 1024

NEG = -1e30
LOG2E = 1.4426950408889634

NORM_ROWS = 512
PROJ_TM, PROJ_TN = 1024, 1024
MID_TM, MID_TN = 1024, 512
OUT_TM, OUT_TN = 512, 1024
CAST_ROWS = 256
NORM_CHUNK = 32
ROPE_ROWS = 512
ATTN_TQ = 256
ATTN_HEADS_PER_STEP = 4
HGRN_ROWS = 1024
HGRN_CHUNK = 128
HGRN_HEADS_PER_STEP = 4
SSD_ROWS = 1024
SSD_CHUNK = 128
SSD_DT_ROWS = 1024


def _silu(x):
    hx = 0.5 * x
    return hx + hx * jnp.tanh(hx)


def _params(*sem):
    return pltpu.CompilerParams(dimension_semantics=sem, vmem_limit_bytes=VMEM_LIMIT)


def _rmsnorm_dt_kernel(x_ref, w_ref, wdt_ref, h_ref, dt_ref):
    x = x_ref[...]
    ms = jnp.mean(x * x, axis=-1, keepdims=True)
    h = ((x * lax.rsqrt(ms + NORM_EPS)) * w_ref[...]).astype(BF16)
    h_ref[...] = h
    dt_ref[...] = jnp.dot(h, wdt_ref[...], preferred_element_type=F32)


def _rmsnorm_dt(x2, w, w_dt):
    m, d = x2.shape
    rows = pl.BlockSpec((NORM_ROWS, d), lambda i: (i, 0))
    return pl.pallas_call(
        _rmsnorm_dt_kernel,
        grid=(m // NORM_ROWS,),
        in_specs=[rows, pl.BlockSpec((1, d), lambda i: (0, 0)),
                  pl.BlockSpec((d, LANES), lambda i: (0, 0))],
        out_specs=[rows, pl.BlockSpec((NORM_ROWS, LANES), lambda i: (i, 0))],
        out_shape=[jax.ShapeDtypeStruct((m, d), BF16), jax.ShapeDtypeStruct((m, LANES), F32)],
        compiler_params=_params("parallel"),
        name="rmsnorm_dt",
    )(x2, w.reshape(1, d), w_dt)


def _inproj_kernel(scaled, h_ref, w_ref, *rest):
    if scaled:
        rs_ref, o_ref, wbf_ref = rest
    else:
        o_ref, wbf_ref = rest

    @pl.when(pl.program_id(1) == 0)
    def _():
        def cast_rows(r, carry):
            rows = pl.ds(pl.multiple_of(r * CAST_ROWS, CAST_ROWS), CAST_ROWS)
            wbf_ref[rows, :] = w_ref[0, rows, :].astype(BF16)
            return carry

        lax.fori_loop(0, wbf_ref.shape[0] // CAST_ROWS, cast_rows, 0)

    prod = lax.dot_general(h_ref[...], wbf_ref[...], (((1,), (1,)), ((), ())),
                           preferred_element_type=F32)
    if scaled:
        prod = prod * jnp.concatenate([rs_ref[...]] * (prod.shape[1] // LANES), axis=1)
    o_ref[...] = prod


def _inproj(h, w_in_t, layer, row_scale=None):
    m, d = h.shape
    scaled = row_scale is not None
    in_specs = [pl.BlockSpec((PROJ_TM, d), lambda j, i: (i, 0)),
                pl.BlockSpec((1, PROJ_TN, d), lambda j, i: (layer, j, 0),
                             pipeline_mode=pl.Buffered(1))]
    args = [h, w_in_t]
    if scaled:
        in_specs.append(pl.BlockSpec((PROJ_TM, LANES), lambda j, i: (i, 0)))
        args.append(row_scale)
    return pl.pallas_call(
        functools.partial(_inproj_kernel, scaled),
        grid=(IN_MAIN // PROJ_TN, m // PROJ_TM),
        in_specs=in_specs,
        out_specs=pl.BlockSpec((PROJ_TM, PROJ_TN), lambda j, i: (i, j)),
        out_shape=jax.ShapeDtypeStruct((m, IN_MAIN), F32),
        scratch_shapes=[pltpu.VMEM((PROJ_TN, d), BF16)],
        compiler_params=_params("parallel", "arbitrary"),
        name="inproj",
    )(*args)


def _outproj_tile(ya_ref, yh_ref, ys_ref, wa_ref, wh_ref, ws_ref, x_ref):
    acc = jnp.dot(ya_ref[...], wa_ref[...], preferred_element_type=F32)
    acc += jnp.dot(yh_ref[...], wh_ref[...], preferred_element_type=F32)
    acc += jnp.dot(ys_ref[...], ws_ref[...], preferred_element_type=F32)
    return x_ref[...] + acc


def _outproj_mid_kernel(ya_ref, yh_ref, ys_ref, wa_ref, wh_ref, ws_ref, x_ref, nw_ref, wdt_ref,
                        xo_ref, hu_ref, rs_ref, dt_ref, ss_ref, dtacc_ref):
    j = pl.program_id(1)
    x_new = _outproj_tile(ya_ref, yh_ref, ys_ref, wa_ref, wh_ref, ws_ref, x_ref)
    xo_ref[...] = x_new
    hu = (x_new * nw_ref[...]).astype(BF16)
    hu_ref[...] = hu
    ss = jnp.sum(x_new * x_new, axis=-1, keepdims=True)
    dt_part = jnp.dot(hu, wdt_ref[...], preferred_element_type=F32)

    @pl.when(j == 0)
    def _():
        ss_ref[...] = jnp.broadcast_to(ss, ss_ref.shape)
        dtacc_ref[...] = dt_part

    @pl.when(j > 0)
    def _():
        ss_ref[...] += jnp.broadcast_to(ss, ss_ref.shape)
        dtacc_ref[...] += dt_part

    @pl.when(j == pl.num_programs(1) - 1)
    def _():
        r = lax.rsqrt(ss_ref[...] / D_MODEL + NORM_EPS)
        rs_ref[...] = r
        dt_ref[...] = dtacc_ref[...] * r


def _outproj_final_kernel(ya_ref, yh_ref, ys_ref, wa_ref, wh_ref, ws_ref, x_ref, nw_ref,
                          out_ref, slab_ref):
    j = pl.program_id(1)
    n_slabs, _, tn = slab_ref.shape
    slab_ref[j] = _outproj_tile(ya_ref, yh_ref, ys_ref, wa_ref, wh_ref, ws_ref, x_ref)

    @pl.when(j == n_slabs - 1)
    def _():
        def norm_rows(r, carry):
            rows = pl.ds(pl.multiple_of(r * NORM_CHUNK, NORM_CHUNK), NORM_CHUNK)
            ss = jnp.sum(slab_ref[0, rows, :] * slab_ref[0, rows, :], axis=-1, keepdims=True)
            for s in range(1, n_slabs):
                ss = ss + jnp.sum(slab_ref[s, rows, :] * slab_ref[s, rows, :], axis=-1, keepdims=True)
            scale = lax.rsqrt(ss / (n_slabs * tn) + NORM_EPS)
            for s in range(n_slabs):
                cols = slice(s * tn, (s + 1) * tn)
                out_ref[rows, cols] = (slab_ref[s, rows, :] * scale) * nw_ref[:, cols]
            return carry

        lax.fori_loop(0, slab_ref.shape[1] // NORM_CHUNK, norm_rows, 0, unroll=2)


def _outproj_specs(tm, tn, layer):
    row = lambda i, j: (i, 0)
    assert HGRN_WIDTH == ATTN_WIDTH and SSM_WIDTH == ATTN_WIDTH + HGRN_WIDTH
    return [pl.BlockSpec((tm, ATTN_WIDTH), row),
            pl.BlockSpec((tm, HGRN_WIDTH), row),
            pl.BlockSpec((tm, SSM_WIDTH), row),
            pl.BlockSpec((None, ATTN_WIDTH, tn), lambda i, j: (layer, 0, j)),
            pl.BlockSpec((None, HGRN_WIDTH, tn), lambda i, j: (layer, 1, j)),
            pl.BlockSpec((None, SSM_WIDTH, tn), lambda i, j: (layer, 1, j)),
            pl.BlockSpec((tm, tn), lambda i, j: (i, j))]


def _outproj_mid(ya, yh, ys, w_out, layer, x2, norm_w, w_dt):
    m, d = x2.shape
    tm, tn = MID_TM, MID_TN
    tile = pl.BlockSpec((tm, tn), lambda i, j: (i, j))
    stat = pl.BlockSpec((tm, LANES), lambda i, j: (i, 0))
    return pl.pallas_call(
        _outproj_mid_kernel,
        grid=(m // tm, d // tn),
        in_specs=_outproj_specs(tm, tn, layer) + [pl.BlockSpec((1, tn), lambda i, j: (0, j)),
                                           pl.BlockSpec((tn, LANES), lambda i, j: (j, 0))],
        out_specs=[tile, tile, stat, stat],
        out_shape=[jax.ShapeDtypeStruct((m, d), F32), jax.ShapeDtypeStruct((m, d), BF16),
                   jax.ShapeDtypeStruct((m, LANES), F32), jax.ShapeDtypeStruct((m, LANES), F32)],
        scratch_shapes=[pltpu.VMEM((tm, LANES), F32), pltpu.VMEM((tm, LANES), F32)],
        compiler_params=_params("parallel", "arbitrary"),
        name="outproj",
    )(ya, yh, ys, w_out, w_out, w_out, x2, norm_w.reshape(1, d), w_dt)


def _outproj_final(ya, yh, ys, w_out, layer, x2, norm_w):
    m, d = x2.shape
    return pl.pallas_call(
        _outproj_final_kernel,
        grid=(m // OUT_TM, d // OUT_TN),
        in_specs=_outproj_specs(OUT_TM, OUT_TN, layer) + [pl.BlockSpec((1, d), lambda i, j: (0, 0))],
        out_specs=pl.BlockSpec((OUT_TM, d), lambda i, j: (i, 0)),
        out_shape=jax.ShapeDtypeStruct((m, d), F32),
        scratch_shapes=[pltpu.VMEM((d // OUT_TN, OUT_TM, OUT_TN), F32)],
        compiler_params=_params("parallel", "arbitrary"),
        name="outproj_final",
    )(ya, yh, ys, w_out, w_out, w_out, x2, norm_w.reshape(1, d))


def _rope_table_kernel(pos_ref, invf_ref, cos_ref, sin_ref):
    ang = pos_ref[0] * invf_ref[...]
    lane = lax.broadcasted_iota(jnp.int32, ang.shape, 1)
    cos_ref[0] = jnp.cos(ang)
    s = jnp.sin(ang)
    sin_ref[0] = jnp.where(lane < ATTN_HEAD_DIM // 2, -s, s)


def _rope_tables(positions):
    b, s = positions.shape
    half = ATTN_HEAD_DIM // 2
    inv_freq = ROPE_THETA ** (-jnp.arange(half, dtype=F32) / half)
    invf = jnp.concatenate([inv_freq, inv_freq]).reshape(1, ATTN_HEAD_DIM)
    posb = jnp.broadcast_to(positions.astype(F32)[..., None], (b, s, ATTN_HEAD_DIM))
    spec = pl.BlockSpec((1, ROPE_ROWS, ATTN_HEAD_DIM), lambda bi, i: (bi, i, 0))
    return pl.pallas_call(
        _rope_table_kernel,
        grid=(b, s // ROPE_ROWS),
        in_specs=[spec, pl.BlockSpec((1, ATTN_HEAD_DIM), lambda bi, i: (0, 0))],
        out_specs=[spec, spec],
        out_shape=[jax.ShapeDtypeStruct((b, s, ATTN_HEAD_DIM), F32)] * 2,
        compiler_params=_params("parallel", "parallel"),
        name="rope_tables",
    )(posb, invf)


def _rope_kernel(pad_blocks, q_ref, k_ref, v_ref, cos_ref, sin_ref, qo_ref, ko_ref, vo_ref):
    i = pl.program_id(1)
    cos = cos_ref[0]
    sin = sin_ref[0]
    half = ATTN_HEAD_DIM // 2
    scale = ATTN_HEAD_DIM ** -0.5 * LOG2E

    def rope(t):
        return t * cos + pltpu.roll(t, half, axis=1) * sin

    for h in range(ATTN_HEADS):
        sl = slice(h * ATTN_HEAD_DIM, (h + 1) * ATTN_HEAD_DIM)
        qo_ref[0, :, sl] = (rope(q_ref[0, :, sl]) * scale).astype(BF16)

    @pl.when(i < pad_blocks)
    def _():
        ko_ref[...] = jnp.zeros_like(ko_ref)
        vo_ref[...] = jnp.zeros_like(vo_ref)

    @pl.when(i >= pad_blocks)
    def _():
        for h in range(ATTN_HEADS):
            sl = slice(h * ATTN_HEAD_DIM, (h + 1) * ATTN_HEAD_DIM)
            ko_ref[0, :, sl] = rope(k_ref[0, :, sl]).astype(BF16)
        vo_ref[0] = v_ref[0].astype(BF16)


def _rope_apply(proj3, cos, sin):
    b, s, _ = proj3.shape
    pad_blocks = ATTN_REACH // ROPE_ROWS
    nblk = s // ROPE_ROWS + pad_blocks
    src = lambda c: (lambda bi, i: (bi, jnp.maximum(i - pad_blocks, 0), c))
    slab = lambda c: pl.BlockSpec((1, ROPE_ROWS, ATTN_WIDTH), src(c))
    tab = pl.BlockSpec((1, ROPE_ROWS, ATTN_HEAD_DIM), src(0))
    return pl.pallas_call(
        functools.partial(_rope_kernel, pad_blocks),
        grid=(b, nblk),
        in_specs=[slab(COL_AQ // ATTN_WIDTH), slab(COL_AK // ATTN_WIDTH), slab(COL_AV // ATTN_WIDTH),
                  tab, tab],
        out_specs=[pl.BlockSpec((1, ROPE_ROWS, ATTN_WIDTH), src(0)),
                   pl.BlockSpec((1, ROPE_ROWS, ATTN_WIDTH), lambda bi, i: (bi, i, 0)),
                   pl.BlockSpec((1, ROPE_ROWS, ATTN_WIDTH), lambda bi, i: (bi, i, 0))],
        out_shape=[jax.ShapeDtypeStruct((b, s, ATTN_WIDTH), BF16),
                   jax.ShapeDtypeStruct((b, s + ATTN_REACH, ATTN_WIDTH), BF16),
                   jax.ShapeDtypeStruct((b, s + ATTN_REACH, ATTN_WIDTH), BF16)],
        compiler_params=_params("parallel", "arbitrary"),
        name="rope_apply",
    )(proj3, proj3, proj3, cos, sin)


def _attn_bias_table(tq):
    r = np.arange(tq)[:, None]
    j = np.arange(ATTN_REACH + tq)[None, :]
    delta = r + ATTN_REACH - j
    count = np.zeros(delta.shape, np.int64)
    for window, dilation in ATTN_PATTERNS:
        count += (delta >= 0) & (delta % dilation == 0) & (delta <= window)
    return np.where(count > 0, np.log2(np.maximum(count, 1)), NEG).astype(np.float32)


def _attn_kernel(q_ref, k_ref, v_ref, g_ref, bias_ref, o_ref):
    i = pl.program_id(2)
    tq = q_ref.shape[1]
    span = ATTN_REACH + tq
    start = pl.multiple_of(i * tq, tq)
    keys = pl.ds(start, span)
    col = lax.broadcasted_iota(jnp.int32, (1, span), 1)
    in_seq = col >= ATTN_REACH - i * tq
    for hh in range(ATTN_HEADS_PER_STEP):
        hs = slice(hh * ATTN_HEAD_DIM, (hh + 1) * ATTN_HEAD_DIM)
        q = q_ref[0, :, hs]
        s = lax.dot_general(q, k_ref[0, keys, hs], (((1,), (1,)), ((), ())),
                            preferred_element_type=F32)
        s = jnp.where(in_seq, s + bias_ref[...], NEG)
        m = jnp.max(s, axis=-1, keepdims=True)
        p = jnp.exp2(s - m)
        den = jnp.sum(p, axis=-1, keepdims=True)
        o = jnp.dot(p.astype(BF16), v_ref[0, keys, hs], preferred_element_type=F32) / den
        g = g_ref[0, :, hs]
        o_ref[0, :, hs] = (o * _silu(g)).astype(o_ref.dtype)


def _attention(q_rot, k_pad, v_pad, proj3):
    b, s, _ = q_rot.shape
    tq = ATTN_TQ
    width = ATTN_HEADS_PER_STEP * ATTN_HEAD_DIM
    bias = jnp.asarray(_attn_bias_table(tq))
    gate0 = COL_AG // width
    full = pl.BlockSpec((1, s + ATTN_REACH, width), lambda bi, h, i: (bi, 0, h),
                        pipeline_mode=pl.Buffered(1))
    return pl.pallas_call(
        _attn_kernel,
        grid=(b, ATTN_HEADS // ATTN_HEADS_PER_STEP, s // tq),
        in_specs=[pl.BlockSpec((1, tq, width), lambda bi, h, i: (bi, i, h)),
                  full, full,
                  pl.BlockSpec((1, tq, width), lambda bi, h, i: (bi, i, gate0 + h)),
                  pl.BlockSpec(bias.shape, lambda bi, h, i: (0, 0))],
        out_specs=pl.BlockSpec((1, tq, width), lambda bi, h, i: (bi, i, h)),
        out_shape=jax.ShapeDtypeStruct((b, s, ATTN_WIDTH), BF16),
        compiler_params=_params("parallel", "parallel", "arbitrary"),
        name="attention",
    )(q_rot, k_pad, v_pad, proj3, bias)


def _split3(x):
    hi = x.astype(BF16)
    r1 = x - hi.astype(F32)
    mid = r1.astype(BF16)
    lo = (r1 - mid.astype(F32)).astype(BF16)
    return hi, mid, lo


def _cumsum_rows_mxu(x, tril):
    n = x.shape[1]
    r = jnp.dot(tril, jnp.concatenate(_split3(x), axis=1), preferred_element_type=F32)
    return (r[:, :n] + r[:, n:2 * n]) + r[:, 2 * n:]


def _log1p_exp_neg_abs(x):
    return jnp.log(1.0 + jnp.exp(-jnp.abs(x)))


def _hgrn_pair_level(c):
    t = lax.broadcasted_iota(jnp.int32, (c, c), 0)
    s = lax.broadcasted_iota(jnp.int32, (c, c), 1)
    x = t ^ s
    level = jnp.full((c, c), -1, jnp.int32)
    m = 1
    while m < c:
        level = level + (x >= m).astype(jnp.int32)
        m *= 2
    return jnp.where(t >= s, level, -2)


def _hgrn_gates(fr, lb, log_lb, log1m_lb, tril):
    log_sig = jnp.minimum(fr, 0.0) - _log1p_exp_neg_abs(fr)
    lower = log1m_lb + log_sig
    g = jnp.maximum(log_lb, lower) + _log1p_exp_neg_abs(log_lb - lower)
    kk = (1.0 - lb) * jax.nn.sigmoid(-fr)
    return kk, _cumsum_rows_mxu(g * LOG2E, tril)


def _hgrn_chunk(q, kk, b, v, st, pair_level):
    c = q.shape[0]
    nt = (((1,), (1,)), ((), ()))
    tn = (((0,), (0,)), ((), ()))
    b_last = b[c - 1:c, :]

    o = lax.dot_general((q * jnp.exp2(b)).astype(BF16), st.astype(BF16), nt,
                        preferred_element_type=F32)

    n = q.shape[1]
    a = jnp.where(pair_level == -1,
                  lax.dot_general(q.astype(BF16), kk.astype(BF16), nt, preferred_element_type=F32), 0.0)
    b3 = b.reshape(c // SUBLANES, SUBLANES, n)
    sub3 = lax.broadcasted_iota(jnp.int32, b3.shape, 1)
    level, m = 0, 1
    while m < c:
        if m >= SUBLANES:
            mid = jnp.concatenate(
                [jnp.broadcast_to(b[blk * 2 * m + m - 1:blk * 2 * m + m, :], (2 * m, n))
                 for blk in range(c // (2 * m))], axis=0)
        else:
            mid3 = jnp.broadcast_to(b3[:, m - 1:m, :], b3.shape)
            for first in range(2 * m, SUBLANES, 2 * m):
                mid3 = jnp.where(sub3 >= first,
                                 jnp.broadcast_to(b3[:, first + m - 1:first + m, :], b3.shape), mid3)
            mid = mid3.reshape(c, n)
        e = jnp.exp2(-jnp.abs(b - mid))
        part = lax.dot_general((q * e).astype(BF16), (kk * e).astype(BF16), nt,
                               preferred_element_type=F32)
        a = jnp.where(pair_level == level, part, a)
        level, m = level + 1, 2 * m
    o = o + jnp.dot(a.astype(BF16), v.astype(BF16), preferred_element_type=F32)

    kd = kk * jnp.exp2(b_last - b)
    st_new = st * jnp.exp2(b_last) + lax.dot_general(v.astype(BF16), kd.astype(BF16), tn,
                                                    preferred_element_type=F32)
    return o, st_new


def _hgrn_kernel(q_ref, f_ref, i_ref, g_ref, lb_ref, llb_ref, l1m_ref, nw_ref, o_ref,
                 st_ref, kk_ref, b_ref):
    @pl.when(pl.program_id(2) == 0)
    def _():
        st_ref[...] = jnp.zeros_like(st_ref)

    rows = q_ref.shape[1]
    ri = lax.broadcasted_iota(jnp.int32, (HGRN_CHUNK, HGRN_CHUNK), 0)
    ci_ = lax.broadcasted_iota(jnp.int32, (HGRN_CHUNK, HGRN_CHUNK), 1)
    tril = (ri >= ci_).astype(BF16)
    pair_level = _hgrn_pair_level(HGRN_CHUNK)

    def gates(ci, carry):
        sl = pl.ds(pl.multiple_of(ci * HGRN_CHUNK, HGRN_CHUNK), HGRN_CHUNK)
        kk, b = _hgrn_gates(f_ref[0, sl, :], lb_ref[...], llb_ref[...], l1m_ref[...], tril)
        kk_ref[sl, :] = kk
        b_ref[sl, :] = b
        return carry

    lax.fori_loop(0, rows // HGRN_CHUNK, gates, 0, unroll=True)

    def body(ci, carry):
        r0 = pl.multiple_of(ci * HGRN_CHUNK, HGRN_CHUNK)
        sl = pl.ds(r0, HGRN_CHUNK)
        for hh in range(HGRN_HEADS_PER_STEP):
            hs = slice(hh * HGRN_DIM, (hh + 1) * HGRN_DIM)
            o, st = _hgrn_chunk(q_ref[0, sl, hs], kk_ref[sl, hs], b_ref[sl, hs], i_ref[0, sl, hs],
                                st_ref[hh], pair_level)
            st_ref[hh] = st
            o = o * lax.rsqrt(jnp.mean(o * o, axis=-1, keepdims=True) + NORM_EPS) * nw_ref[:, hs]
            g = g_ref[0, sl, hs]
            o_ref[0, sl, hs] = (o * _silu(g)).astype(o_ref.dtype)
        return carry

    lax.fori_loop(0, rows // HGRN_CHUNK, body, 0, unroll=True)


def _hgrn(proj3, lb, norm_w):
    b, s, _ = proj3.shape
    lb = lb.reshape(1, HGRN_WIDTH)
    log_lb = jnp.log(lb)
    log1m_lb = jnp.log1p(-lb)
    width = HGRN_HEADS_PER_STEP * HGRN_DIM
    blk = lambda c0: pl.BlockSpec((1, HGRN_ROWS, width), lambda bi, h, t: (bi, t, c0 // width + h))
    par = pl.BlockSpec((1, width), lambda bi, h, t: (0, h))
    return pl.pallas_call(
        _hgrn_kernel,
        grid=(b, HGRN_HEADS // HGRN_HEADS_PER_STEP, s // HGRN_ROWS),
        in_specs=[blk(COL_HQ), blk(COL_HF), blk(COL_HI), blk(COL_HG), par, par, par, par],
        out_specs=pl.BlockSpec((1, HGRN_ROWS, width), lambda bi, h, t: (bi, t, h)),
        out_shape=jax.ShapeDtypeStruct((b, s, HGRN_WIDTH), BF16),
        scratch_shapes=[pltpu.VMEM((HGRN_HEADS_PER_STEP, HGRN_DIM, HGRN_DIM), F32),
                        pltpu.VMEM((HGRN_ROWS, width), F32),
                        pltpu.VMEM((HGRN_ROWS, width), F32)],
        compiler_params=_params("parallel", "parallel", "arbitrary"),
        name="hgrn2",
    )(proj3, proj3, proj3, proj3, lb, log_lb, log1m_lb, norm_w.reshape(1, HGRN_WIDTH))


def _softplus(x):
    return jnp.maximum(x, 0.0) + _log1p_exp_neg_abs(x)


def _cumsum_lanes(x):
    n = x.shape[1]
    lane = lax.broadcasted_iota(jnp.int32, x.shape, 1)
    k = 1
    while k < n:
        x = x + jnp.where(lane >= k, pltpu.roll(x, k, axis=1), 0.0)
        k *= 2
    return x


def _ssd_dt_kernel(dt_ref, bias_ref, alog_ref, acrow_ref, accol_ref, dtcol_ref):
    heads, lc = SSM_HEADS, SSD_CHUNK
    neg_a = jnp.exp(alog_ref[...])
    zeros = jnp.zeros((LANES - 3 * heads, lc), F32)
    for c in range(dt_ref.shape[1] // lc):
        sl = slice(c * lc, (c + 1) * lc)
        raw = dt_ref[0, sl, :].T[:heads]
        dt = _softplus(raw + bias_ref[...])
        ac = _cumsum_lanes(dt * (-LOG2E * neg_a))
        acrow_ref[0, :, sl] = ac
        for val, dst in ((ac, accol_ref), (dt, dtcol_ref)):
            pieces = [p.astype(F32) for p in _split3(val)]
            dst[0, sl, :] = jnp.concatenate(pieces + [zeros], axis=0).T.astype(BF16)


def _ssd_dt(dt3, dt_bias, a_log):
    b, s, _ = dt3.shape
    rows = SSD_DT_ROWS
    col = pl.BlockSpec((1, rows, LANES), lambda bi, t: (bi, t, 0))
    par = pl.BlockSpec((SSM_HEADS, 1), lambda bi, t: (0, 0))
    return pl.pallas_call(
        _ssd_dt_kernel,
        grid=(b, s // rows),
        in_specs=[col, par, par],
        out_specs=[pl.BlockSpec((1, SSM_HEADS, rows), lambda bi, t: (bi, 0, t)), col, col],
        out_shape=[jax.ShapeDtypeStruct((b, SSM_HEADS, s), F32),
                   jax.ShapeDtypeStruct((b, s, LANES), BF16),
                   jax.ShapeDtypeStruct((b, s, LANES), BF16)],
        compiler_params=_params("parallel", "parallel"),
        name="ssd_dt",
    )(dt3, dt_bias.reshape(SSM_HEADS, 1), a_log.reshape(SSM_HEADS, 1))


SEL_AC_WIDE = SSM_HEADS_PER_GROUP * LANES
SEL_AC_X = SEL_AC_WIDE + SSM_GROUP_WIDTH
SEL_WIDTH = SEL_AC_X + SSM_GROUP_WIDTH


def _ssd_select_matrices():
    sel = np.zeros((SSM_GROUPS, 2 * LANES, SEL_WIDTH), np.float32)
    for g in range(SSM_GROUPS):
        for e in range(SSM_HEADS_PER_GROUP):
            head = g * SSM_HEADS_PER_GROUP + e
            for piece in range(3):
                r = piece * SSM_HEADS + head
                sel[g, r, e * LANES:(e + 1) * LANES] = 1.0
                sel[g, r, SEL_AC_WIDE + e * SSM_HEAD_DIM:SEL_AC_WIDE + (e + 1) * SSM_HEAD_DIM] = 1.0
                sel[g, LANES + r, SEL_AC_X + e * SSM_HEAD_DIM:SEL_AC_X + (e + 1) * SSM_HEAD_DIM] = 1.0
    return sel


def _ssd_kernel(z_ref, x_ref, b_ref, c_ref, acrow_ref, accol_ref, dtcol_ref, sel_ref,
                cw_ref, cb_ref, dsk_ref, nw_ref, o_ref, st_ref, stage_ref, xbc_ref, acr_ref):
    e_heads, p_dim, lc = SSM_HEADS_PER_GROUP, SSM_HEAD_DIM, SSD_CHUNK
    rows = x_ref.shape[1]
    xw, sw = SSM_GROUP_WIDTH, SSM_STATE
    n_slabs = (xw + 2 * sw) // LANES

    @pl.when(pl.program_id(2) == 0)
    def _():
        st_ref[...] = jnp.zeros_like(st_ref)
        stage_ref[:, :SUBLANES, :] = jnp.zeros((n_slabs, SUBLANES, LANES), F32)

    for slab in range(xw // LANES):
        stage_ref[slab, SUBLANES:, :] = x_ref[0, :, slab * LANES:(slab + 1) * LANES]
    stage_ref[xw // LANES, SUBLANES:, :] = b_ref[0]
    stage_ref[xw // LANES + 1, SUBLANES:, :] = c_ref[0]
    for ci in range(rows // lc):
        acr_ref[ci] = acrow_ref[0, 0, :, ci * lc:(ci + 1) * lc]

    li = lax.broadcasted_iota(jnp.int32, (lc, lc), 0)
    si = lax.broadcasted_iota(jnp.int32, (lc, lc), 1)
    causal = li >= si
    lane_head = lax.broadcasted_iota(jnp.int32, (1, xw), 1) // p_dim
    nt = (((1,), (1,)), ((), ()))
    tn = (((0,), (0,)), ((), ()))
    cw = cw_ref[0]
    cbias = cb_ref[0]

    def body(ci, carry):
        r0 = pl.multiple_of(ci * lc, lc)
        sl = pl.ds(r0, lc)
        for slab in range(n_slabs):
            lanes = slice(slab * LANES, (slab + 1) * LANES)
            for parity in range(2):
                acc = cbias[:, lanes]
                for j in range(SSM_CONV):
                    first = r0 + (SUBLANES - (SSM_CONV - 1) + j + parity)
                    tap = stage_ref[slab, pl.ds(first, lc // 2, stride=2), :]
                    acc = acc + tap * cw[j:j + 1, lanes]
                xbc_ref[slab, pl.ds(r0 + parity, lc // 2, stride=2), :] = _silu(acc)
        xs = jnp.concatenate([xbc_ref[slab, sl, :] for slab in range(xw // LANES)], axis=1)
        bm = xbc_ref[xw // LANES, sl, :].astype(BF16)
        cm = xbc_ref[xw // LANES + 1, sl, :].astype(BF16)

        pieces = jnp.concatenate([accol_ref[0, sl, :], dtcol_ref[0, sl, :]], axis=1)
        bc = jnp.dot(pieces, sel_ref[0], preferred_element_type=F32)
        ac_x = bc[:, SEL_AC_WIDE:SEL_AC_X]
        xdt = xs * bc[:, SEL_AC_X:]
        ac_last = ac_x[lc - 1:lc, :]
        ac_r = acr_ref[ci]

        cb = lax.dot_general(cm, bm, nt, preferred_element_type=F32)
        decayed, x_heads = [], []
        for e in range(e_heads):
            dif = bc[:, e * LANES:(e + 1) * LANES] - ac_r[e:e + 1, :]
            decayed.append((cb * jnp.exp2(jnp.where(causal, dif, NEG))).astype(BF16))
            x_heads.append(jnp.where(lane_head == e, xdt, 0.0).astype(BF16))
        y = jnp.dot(jnp.concatenate(decayed, axis=1), jnp.concatenate(x_heads, axis=0),
                    preferred_element_type=F32)
        st = st_ref[...]
        y = y + jnp.dot(cm, st.astype(BF16), preferred_element_type=F32) * jnp.exp2(ac_x)
        xdec = (xdt * jnp.exp2(ac_last - ac_x)).astype(BF16)
        st_ref[...] = st * jnp.exp2(ac_last) + lax.dot_general(bm, xdec, tn,
                                                              preferred_element_type=F32)
        y = y + xs * dsk_ref[...]
        y = y * _silu(z_ref[0, sl, :])
        y = y * lax.rsqrt(jnp.mean(y * y, axis=-1, keepdims=True) + NORM_EPS) * nw_ref[...]
        o_ref[0, sl, :] = y.astype(o_ref.dtype)
        return carry

    lax.fori_loop(0, rows // lc, body, 0, unroll=True)
    stage_ref[:, :SUBLANES, :] = stage_ref[:, rows:rows + SUBLANES, :]


def _ssd(proj3, dt3, conv_w, conv_b, dt_bias, a_log, d_skip, norm_w):
    b, s, _ = proj3.shape
    g, e = SSM_GROUPS, SSM_HEADS_PER_GROUP
    rows = SSD_ROWS
    ac_row, ac_col, dt_col = _ssd_dt(dt3, dt_bias, a_log)
    ac_row = ac_row.reshape(b, g, e, s)
    sel = jnp.asarray(_ssd_select_matrices(), BF16)

    def per_group(p):
        px = p[:, :SSM_WIDTH].reshape(-1, g, SSM_GROUP_WIDTH)
        pb = p[:, SSM_WIDTH:SSM_WIDTH + g * SSM_STATE].reshape(-1, g, SSM_STATE)
        pc = p[:, SSM_WIDTH + g * SSM_STATE:].reshape(-1, g, SSM_STATE)
        return jnp.concatenate([px, pb, pc], axis=2).transpose(1, 0, 2)

    conv_width = SSM_GROUP_WIDTH + 2 * SSM_STATE
    dsk = jnp.repeat(d_skip, SSM_HEAD_DIM).reshape(1, SSM_WIDTH)

    def act(c0, width):
        return pl.BlockSpec((1, rows, width), lambda bi, gi, t: (bi, t, c0 // width + gi))

    col = pl.BlockSpec((1, rows, LANES), lambda bi, gi, t: (bi, t, 0))
    grp = lambda nrows, width: pl.BlockSpec((1, nrows, width), lambda bi, gi, t: (gi, 0, 0))
    par = pl.BlockSpec((1, SSM_GROUP_WIDTH), lambda bi, gi, t: (0, gi))
    return pl.pallas_call(
        _ssd_kernel,
        grid=(b, g, s // rows),
        in_specs=[act(COL_SZ, SSM_GROUP_WIDTH), act(COL_SX, SSM_GROUP_WIDTH),
                  act(COL_SB, SSM_STATE), act(COL_SC, SSM_STATE),
                  pl.BlockSpec((1, 1, e, rows), lambda bi, gi, t: (bi, gi, 0, t)),
                  col, col, grp(2 * LANES, SEL_WIDTH),
                  grp(SSM_CONV, conv_width), grp(1, conv_width), par, par],
        out_specs=pl.BlockSpec((1, rows, SSM_GROUP_WIDTH), lambda bi, gi, t: (bi, t, gi)),
        out_shape=jax.ShapeDtypeStruct((b, s, SSM_WIDTH), BF16),
        scratch_shapes=[pltpu.VMEM((SSM_STATE, SSM_GROUP_WIDTH), F32),
                        pltpu.VMEM((conv_width // LANES, SUBLANES + rows, LANES), F32),
                        pltpu.VMEM((conv_width // LANES, rows, LANES), F32),
                        pltpu.VMEM((rows // SSD_CHUNK, e, SSD_CHUNK), F32)],
        compiler_params=_params("parallel", "parallel", "arbitrary"),
        name="ssd",
    )(proj3, proj3, proj3, proj3, ac_row, ac_col, dt_col, sel,
      per_group(conv_w), per_group(conv_b.reshape(1, -1)), dsk, norm_w.reshape(1, SSM_WIDTH))


def _mixers(h, row_scale, dt, batch, cos, sin, w_in_t, layer, conv_w, conv_b, dt_bias, a_log, d_skip,
            hgrn_norm_w, ssm_norm_w, lb):
    m = h.shape[0]
    s = m // batch
    proj3 = _inproj(h, w_in_t, layer, row_scale).reshape(batch, s, IN_MAIN)
    dt3 = dt.reshape(batch, s, LANES)
    q_rot, k_pad, v_pad = _rope_apply(proj3, cos, sin)
    ya = _attention(q_rot, k_pad, v_pad, proj3)
    yh = _hgrn(proj3, lb, hgrn_norm_w)
    ys = _ssd(proj3, dt3, conv_w, conv_b, dt_bias, a_log, d_skip, ssm_norm_w)
    return ya.reshape(m, -1), yh.reshape(m, -1), ys.reshape(m, -1)


def kernel(x, positions, norm_w, w_in, conv_w, conv_b, dt_bias, a_log, d_skip, hgrn_norm_w,
           ssm_norm_w, w_out, hgrn_lb_logits, final_norm_w):
    batch, s, d = x.shape
    p = jax.nn.softmax(hgrn_lb_logits.astype(F32), axis=0)
    cs = jnp.cumsum(p, axis=0)
    lb_all = cs - cs[0:1]
    w_dt = jnp.pad(w_in[:, :, IN_MAIN:], ((0, 0), (0, 0), (0, LANES - SSM_HEADS))).astype(BF16)
    w_in_t = jnp.swapaxes(w_in, 1, 2)
    w_out_bf = w_out.astype(BF16)
    cos, sin = _rope_tables(positions)
    x2 = x.reshape(batch * s, d)
    h, dt = _rmsnorm_dt(x2, norm_w[0], w_dt[0])
    row_scale = None
    for l in range(DEPTH):
        ya, yh, ys = _mixers(h, row_scale, dt, batch, cos, sin, w_in_t, l, conv_w[l], conv_b[l],
                             dt_bias[l], a_log[l], d_skip[l], hgrn_norm_w[l], ssm_norm_w[l], lb_all[l])
        if l + 1 < DEPTH:
            x2, h, row_scale, dt = _outproj_mid(ya, yh, ys, w_out_bf, l, x2, norm_w[l + 1], w_dt[l + 1])
        else:
            out = _outproj_final(ya, yh, ys, w_out_bf, l, x2, final_norm_w)
    return out.reshape(batch, s, d)
```

```python
import functools

import numpy as np
import jax
import jax.numpy as jnp
from jax import lax
from jax.experimental import pallas as pl
from jax.experimental.pallas import tpu as pltpu

F32 = jnp.float32
BF16 = jnp.bfloat16

D_MODEL = 4096
DEPTH = 2
NORM_EPS = 1e-6
ATTN_HEADS = 8
ATTN_HEAD_DIM = 128
ATTN_WIDTH = ATTN_HEADS * ATTN_HEAD_DIM
ATTN_PATTERNS = ((128, 1), (512, 4), (2048, 16))
ATTN_REACH = max(w for w, _ in ATTN_PATTERNS)
ROPE_THETA = 10000.0
HGRN_HEADS = 8
HGRN_DIM = 128
HGRN_WIDTH = HGRN_HEADS * HGRN_DIM
SSM_HEADS = 32
SSM_HEAD_DIM = 64
SSM_WIDTH = SSM_HEADS * SSM_HEAD_DIM
SSM_GROUPS = 8
SSM_HEADS_PER_GROUP = SSM_HEADS // SSM_GROUPS
SSM_GROUP_WIDTH = SSM_WIDTH // SSM_GROUPS
SSM_STATE = 128
SSM_CONV = 4
MIX_WIDTH = ATTN_WIDTH + HGRN_WIDTH + SSM_WIDTH

COL_AQ, COL_AK, COL_AV, COL_AG = 0, 1024, 2048, 3072
COL_HQ, COL_HF, COL_HI, COL_HG = 4096, 5120, 6144, 7168
COL_SZ = 8192
COL_SX = 10240
COL_SB = COL_SX + SSM_WIDTH
COL_SC = COL_SB + SSM_GROUPS * SSM_STATE
COL_DT = COL_SC + SSM_GROUPS * SSM_STATE
IN_MAIN = COL_DT

LANES = 128
SUBLANES = 8
VMEM_LIMIT = 60 * 1024 * 1024

NEG = -1e30
LOG2E = 1.4426950408889634

NORM_ROWS = 512
PROJ_TM, PROJ_TN = 1024, 1024
MID_TM, MID_TN = 1024, 512
OUT_TM, OUT_TN = 512, 1024
CAST_ROWS = 256
NORM_CHUNK = 32
ROPE_ROWS = 512
ATTN_TQ = 256
ATTN_HEADS_PER_STEP = 4
HGRN_ROWS = 1024
HGRN_CHUNK = 256
HGRN_HEADS_PER_STEP = 4
SSD_ROWS = 1024
SSD_CHUNK = 128
SSD_DT_ROWS = 1024


def _silu(x):
    hx = 0.5 * x
    return hx + hx * jnp.tanh(hx)


def _params(*sem):
    return pltpu.CompilerParams(dimension_semantics=sem, vmem_limit_bytes=VMEM_LIMIT)


def _rmsnorm_dt_kernel(x_ref, w_ref, wdt_ref, h_ref, dt_ref):
    x = x_ref[...]
    ms = jnp.mean(x * x, axis=-1, keepdims=True)
    h = ((x * lax.rsqrt(ms + NORM_EPS)) * w_ref[...]).astype(BF16)
    h_ref[...] = h
    dt_ref[...] = jnp.dot(h, wdt_ref[...], preferred_element_type=F32)


def _rmsnorm_dt(x2, w, w_dt):
    m, d = x2.shape
    rows = pl.BlockSpec((NORM_ROWS, d), lambda i: (i, 0))
    return pl.pallas_call(
        _rmsnorm_dt_kernel,
        grid=(m // NORM_ROWS,),
        in_specs=[rows, pl.BlockSpec((1, d), lambda i: (0, 0)),
                  pl.BlockSpec((d, LANES), lambda i: (0, 0))],
        out_specs=[rows, pl.BlockSpec((NORM_ROWS, LANES), lambda i: (i, 0))],
        out_shape=[jax.ShapeDtypeStruct((m, d), BF16), jax.ShapeDtypeStruct((m, LANES), F32)],
        compiler_params=_params("parallel"),
        name="rmsnorm_dt",
    )(x2, w.reshape(1, d), w_dt)


def _inproj_kernel(scaled, h_ref, w_ref, *rest):
    if scaled:
        rs_ref, o_ref, wbf_ref = rest
    else:
        o_ref, wbf_ref = rest

    @pl.when(pl.program_id(1) == 0)
    def _():
        def cast_rows(r, carry):
            rows = pl.ds(pl.multiple_of(r * CAST_ROWS, CAST_ROWS), CAST_ROWS)
            wbf_ref[rows, :] = w_ref[0, rows, :].astype(BF16)
            return carry

        lax.fori_loop(0, wbf_ref.shape[0] // CAST_ROWS, cast_rows, 0)

    prod = lax.dot_general(h_ref[...], wbf_ref[...], (((1,), (1,)), ((), ())),
                           preferred_element_type=F32)
    if scaled:
        prod = prod * jnp.concatenate([rs_ref[...]] * (prod.shape[1] // LANES), axis=1)
    o_ref[...] = prod


def _inproj(h, w_in_t, layer, row_scale=None):
    m, d = h.shape
    scaled = row_scale is not None
    in_specs = [pl.BlockSpec((PROJ_TM, d), lambda j, i: (i, 0)),
                pl.BlockSpec((1, PROJ_TN, d), lambda j, i: (layer, j, 0),
                             pipeline_mode=pl.Buffered(1))]
    args = [h, w_in_t]
    if scaled:
        in_specs.append(pl.BlockSpec((PROJ_TM, LANES), lambda j, i: (i, 0)))
        args.append(row_scale)
    return pl.pallas_call(
        functools.partial(_inproj_kernel, scaled),
        grid=(IN_MAIN // PROJ_TN, m // PROJ_TM),
        in_specs=in_specs,
        out_specs=pl.BlockSpec((PROJ_TM, PROJ_TN), lambda j, i: (i, j)),
        out_shape=jax.ShapeDtypeStruct((m, IN_MAIN), F32),
        scratch_shapes=[pltpu.VMEM((PROJ_TN, d), BF16)],
        compiler_params=_params("parallel", "arbitrary"),
        name="inproj",
    )(*args)


def _outproj_tile(ya_ref, yh_ref, ys_ref, wa_ref, wh_ref, ws_ref, x_ref):
    acc = jnp.dot(ya_ref[...], wa_ref[...], preferred_element_type=F32)
    acc += jnp.dot(yh_ref[...], wh_ref[...], preferred_element_type=F32)
    acc += jnp.dot(ys_ref[...], ws_ref[...], preferred_element_type=F32)
    return x_ref[...] + acc


def _outproj_mid_kernel(ya_ref, yh_ref, ys_ref, wa_ref, wh_ref, ws_ref, x_ref, nw_ref, wdt_ref,
                        xo_ref, hu_ref, rs_ref, dt_ref, ss_ref, dtacc_ref):
    j = pl.program_id(1)
    x_new = _outproj_tile(ya_ref, yh_ref, ys_ref, wa_ref, wh_ref, ws_ref, x_ref)
    xo_ref[...] = x_new
    hu = (x_new * nw_ref[...]).astype(BF16)
    hu_ref[...] = hu
    ss = jnp.sum(x_new * x_new, axis=-1, keepdims=True)
    dt_part = jnp.dot(hu, wdt_ref[...], preferred_element_type=F32)

    @pl.when(j == 0)
    def _():
        ss_ref[...] = jnp.broadcast_to(ss, ss_ref.shape)
        dtacc_ref[...] = dt_part

    @pl.when(j > 0)
    def _():
        ss_ref[...] += jnp.broadcast_to(ss, ss_ref.shape)
        dtacc_ref[...] += dt_part

    @pl.when(j == pl.num_programs(1) - 1)
    def _():
        r = lax.rsqrt(ss_ref[...] / D_MODEL + NORM_EPS)
        rs_ref[...] = r
        dt_ref[...] = dtacc_ref[...] * r


def _outproj_final_kernel(ya_ref, yh_ref, ys_ref, wa_ref, wh_ref, ws_ref, x_ref, nw_ref,
                          out_ref, slab_ref):
    j = pl.program_id(1)
    n_slabs, _, tn = slab_ref.shape
    slab_ref[j] = _outproj_tile(ya_ref, yh_ref, ys_ref, wa_ref, wh_ref, ws_ref, x_ref)

    @pl.when(j == n_slabs - 1)
    def _():
        def norm_rows(r, carry):
            rows = pl.ds(pl.multiple_of(r * NORM_CHUNK, NORM_CHUNK), NORM_CHUNK)
            ss = jnp.sum(slab_ref[0, rows, :] * slab_ref[0, rows, :], axis=-1, keepdims=True)
            for s in range(1, n_slabs):
                ss = ss + jnp.sum(slab_ref[s, rows, :] * slab_ref[s, rows, :], axis=-1, keepdims=True)
            scale = lax.rsqrt(ss / (n_slabs * tn) + NORM_EPS)
            for s in range(n_slabs):
                cols = slice(s * tn, (s + 1) * tn)
                out_ref[rows, cols] = (slab_ref[s, rows, :] * scale) * nw_ref[:, cols]
            return carry

        lax.fori_loop(0, slab_ref.shape[1] // NORM_CHUNK, norm_rows, 0, unroll=2)


def _outproj_specs(tm, tn, layer):
    row = lambda i, j: (i, 0)
    assert HGRN_WIDTH == ATTN_WIDTH and SSM_WIDTH == ATTN_WIDTH + HGRN_WIDTH
    return [pl.BlockSpec((tm, ATTN_WIDTH), row),
            pl.BlockSpec((tm, HGRN_WIDTH), row),
            pl.BlockSpec((tm, SSM_WIDTH), row),
            pl.BlockSpec((None, ATTN_WIDTH, tn), lambda i, j: (layer, 0, j)),
            pl.BlockSpec((None, HGRN_WIDTH, tn), lambda i, j: (layer, 1, j)),
            pl.BlockSpec((None, SSM_WIDTH, tn), lambda i, j: (layer, 1, j)),
            pl.BlockSpec((tm, tn), lambda i, j: (i, j))]


def _outproj_mid(ya, yh, ys, w_out, layer, x2, norm_w, w_dt):
    m, d = x2.shape
    tm, tn = MID_TM, MID_TN
    tile = pl.BlockSpec((tm, tn), lambda i, j: (i, j))
    stat = pl.BlockSpec((tm, LANES), lambda i, j: (i, 0))
    return pl.pallas_call(
        _outproj_mid_kernel,
        grid=(m // tm, d // tn),
        in_specs=_outproj_specs(tm, tn, layer) + [pl.BlockSpec((1, tn), lambda i, j: (0, j)),
                                           pl.BlockSpec((tn, LANES), lambda i, j: (j, 0))],
        out_specs=[tile, tile, stat, stat],
        out_shape=[jax.ShapeDtypeStruct((m, d), F32), jax.ShapeDtypeStruct((m, d), BF16),
                   jax.ShapeDtypeStruct((m, LANES), F32), jax.ShapeDtypeStruct((m, LANES), F32)],
        scratch_shapes=[pltpu.VMEM((tm, LANES), F32), pltpu.VMEM((tm, LANES), F32)],
        compiler_params=_params("parallel", "arbitrary"),
        name="outproj",
    )(ya, yh, ys, w_out, w_out, w_out, x2, norm_w.reshape(1, d), w_dt)


def _outproj_final(ya, yh, ys, w_out, layer, x2, norm_w):
    m, d = x2.shape
    return pl.pallas_call(
        _outproj_final_kernel,
        grid=(m // OUT_TM, d // OUT_TN),
        in_specs=_outproj_specs(OUT_TM, OUT_TN, layer) + [pl.BlockSpec((1, d), lambda i, j: (0, 0))],
        out_specs=pl.BlockSpec((OUT_TM, d), lambda i, j: (i, 0)),
        out_shape=jax.ShapeDtypeStruct((m, d), F32),
        scratch_shapes=[pltpu.VMEM((d // OUT_TN, OUT_TM, OUT_TN), F32)],
        compiler_params=_params("parallel", "arbitrary"),
        name="outproj_final",
    )(ya, yh, ys, w_out, w_out, w_out, x2, norm_w.reshape(1, d))


def _rope_table_kernel(pos_ref, invf_ref, cos_ref, sin_ref):
    ang = pos_ref[0] * invf_ref[...]
    lane = lax.broadcasted_iota(jnp.int32, ang.shape, 1)
    cos_ref[0] = jnp.cos(ang)
    s = jnp.sin(ang)
    sin_ref[0] = jnp.where(lane < ATTN_HEAD_DIM // 2, -s, s)


def _rope_tables(positions):
    b, s = positions.shape
    half = ATTN_HEAD_DIM // 2
    inv_freq = ROPE_THETA ** (-jnp.arange(half, dtype=F32) / half)
    invf = jnp.concatenate([inv_freq, inv_freq]).reshape(1, ATTN_HEAD_DIM)
    posb = jnp.broadcast_to(positions.astype(F32)[..., None], (b, s, ATTN_HEAD_DIM))
    spec = pl.BlockSpec((1, ROPE_ROWS, ATTN_HEAD_DIM), lambda bi, i: (bi, i, 0))
    return pl.pallas_call(
        _rope_table_kernel,
        grid=(b, s // ROPE_ROWS),
        in_specs=[spec, pl.BlockSpec((1, ATTN_HEAD_DIM), lambda bi, i: (0, 0))],
        out_specs=[spec, spec],
        out_shape=[jax.ShapeDtypeStruct((b, s, ATTN_HEAD_DIM), F32)] * 2,
        compiler_params=_params("parallel", "parallel"),
        name="rope_tables",
    )(posb, invf)


def _rope_kernel(pad_blocks, q_ref, k_ref, v_ref, cos_ref, sin_ref, qo_ref, ko_ref, vo_ref):
    i = pl.program_id(1)
    cos = cos_ref[0]
    sin = sin_ref[0]
    half = ATTN_HEAD_DIM // 2
    scale = ATTN_HEAD_DIM ** -0.5 * LOG2E

    def rope(t):
        return t * cos + pltpu.roll(t, half, axis=1) * sin

    for h in range(ATTN_HEADS):
        sl = slice(h * ATTN_HEAD_DIM, (h + 1) * ATTN_HEAD_DIM)
        qo_ref[0, :, sl] = (rope(q_ref[0, :, sl]) * scale).astype(BF16)

    @pl.when(i < pad_blocks)
    def _():
        ko_ref[...] = jnp.zeros_like(ko_ref)
        vo_ref[...] = jnp.zeros_like(vo_ref)

    @pl.when(i >= pad_blocks)
    def _():
        for h in range(ATTN_HEADS):
            sl = slice(h * ATTN_HEAD_DIM, (h + 1) * ATTN_HEAD_DIM)
            ko_ref[0, :, sl] = rope(k_ref[0, :, sl]).astype(BF16)
        vo_ref[0] = v_ref[0].astype(BF16)


def _rope_apply(proj3, cos, sin):
    b, s, _ = proj3.shape
    pad_blocks = ATTN_REACH // ROPE_ROWS
    nblk = s // ROPE_ROWS + pad_blocks
    src = lambda c: (lambda bi, i: (bi, jnp.maximum(i - pad_blocks, 0), c))
    slab = lambda c: pl.BlockSpec((1, ROPE_ROWS, ATTN_WIDTH), src(c))
    tab = pl.BlockSpec((1, ROPE_ROWS, ATTN_HEAD_DIM), src(0))
    return pl.pallas_call(
        functools.partial(_rope_kernel, pad_blocks),
        grid=(b, nblk),
        in_specs=[slab(COL_AQ // ATTN_WIDTH), slab(COL_AK // ATTN_WIDTH), slab(COL_AV // ATTN_WIDTH),
                  tab, tab],
        out_specs=[pl.BlockSpec((1, ROPE_ROWS, ATTN_WIDTH), src(0)),
                   pl.BlockSpec((1, ROPE_ROWS, ATTN_WIDTH), lambda bi, i: (bi, i, 0)),
                   pl.BlockSpec((1, ROPE_ROWS, ATTN_WIDTH), lambda bi, i: (bi, i, 0))],
        out_shape=[jax.ShapeDtypeStruct((b, s, ATTN_WIDTH), BF16),
                   jax.ShapeDtypeStruct((b, s + ATTN_REACH, ATTN_WIDTH), BF16),
                   jax.ShapeDtypeStruct((b, s + ATTN_REACH, ATTN_WIDTH), BF16)],
        compiler_params=_params("parallel", "arbitrary"),
        name="rope_apply",
    )(proj3, proj3, proj3, cos, sin)


def _attn_bias_table(tq):
    r = np.arange(tq)[:, None]
    j = np.arange(ATTN_REACH + tq)[None, :]
    delta = r + ATTN_REACH - j
    count = np.zeros(delta.shape, np.int64)
    for window, dilation in ATTN_PATTERNS:
        count += (delta >= 0) & (delta % dilation == 0) & (delta <= window)
    return np.where(count > 0, np.log2(np.maximum(count, 1)), NEG).astype(np.float32)


def _attn_kernel(q_ref, k_ref, v_ref, g_ref, bias_ref, o_ref):
    i = pl.program_id(2)
    tq = q_ref.shape[1]
    span = ATTN_REACH + tq
    start = pl.multiple_of(i * tq, tq)
    keys = pl.ds(start, span)
    col = lax.broadcasted_iota(jnp.int32, (1, span), 1)
    in_seq = col >= ATTN_REACH - i * tq
    for hh in range(ATTN_HEADS_PER_STEP):
        hs = slice(hh * ATTN_HEAD_DIM, (hh + 1) * ATTN_HEAD_DIM)
        q = q_ref[0, :, hs]
        s = lax.dot_general(q, k_ref[0, keys, hs], (((1,), (1,)), ((), ())),
                            preferred_element_type=F32)
        s = jnp.where(in_seq, s + bias_ref[...], NEG)
        m = jnp.max(s, axis=-1, keepdims=True)
        p = jnp.exp2(s - m)
        den = jnp.sum(p, axis=-1, keepdims=True)
        o = jnp.dot(p.astype(BF16), v_ref[0, keys, hs], preferred_element_type=F32) / den
        g = g_ref[0, :, hs]
        o_ref[0, :, hs] = (o * _silu(g)).astype(o_ref.dtype)


def _attention(q_rot, k_pad, v_pad, proj3):
    b, s, _ = q_rot.shape
    tq = ATTN_TQ
    width = ATTN_HEADS_PER_STEP * ATTN_HEAD_DIM
    bias = jnp.asarray(_attn_bias_table(tq))
    gate0 = COL_AG // width
    full = pl.BlockSpec((1, s + ATTN_REACH, width), lambda bi, h, i: (bi, 0, h),
                        pipeline_mode=pl.Buffered(1))
    return pl.pallas_call(
        _attn_kernel,
        grid=(b, ATTN_HEADS // ATTN_HEADS_PER_STEP, s // tq),
        in_specs=[pl.BlockSpec((1, tq, width), lambda bi, h, i: (bi, i, h)),
                  full, full,
                  pl.BlockSpec((1, tq, width), lambda bi, h, i: (bi, i, gate0 + h)),
                  pl.BlockSpec(bias.shape, lambda bi, h, i: (0, 0))],
        out_specs=pl.BlockSpec((1, tq, width), lambda bi, h, i: (bi, i, h)),
        out_shape=jax.ShapeDtypeStruct((b, s, ATTN_WIDTH), BF16),
        compiler_params=_params("parallel", "parallel", "arbitrary"),
        name="attention",
    )(q_rot, k_pad, v_pad, proj3, bias)


def _split3(x):
    hi = x.astype(BF16)
    r1 = x - hi.astype(F32)
    mid = r1.astype(BF16)
    lo = (r1 - mid.astype(F32)).astype(BF16)
    return hi, mid, lo


def _cumsum_rows_mxu(x, tril):
    n = x.shape[1]
    r = jnp.dot(tril, jnp.concatenate(_split3(x), axis=1), preferred_element_type=F32)
    return (r[:, :n] + r[:, n:2 * n]) + r[:, 2 * n:]


def _log1p_exp_neg_abs(x):
    return jnp.log(1.0 + jnp.exp(-jnp.abs(x)))


def _hgrn_pair_level(c):
    t = lax.broadcasted_iota(jnp.int32, (c, c), 0)
    s = lax.broadcasted_iota(jnp.int32, (c, c), 1)
    x = t ^ s
    level = jnp.full((c, c), -1, jnp.int32)
    m = 1
    while m < c:
        level = level + (x >= m).astype(jnp.int32)
        m *= 2
    return jnp.where(t >= s, level, -2)


def _hgrn_gates(fr, lb, log_lb, log1m_lb, tril):
    log_sig = jnp.minimum(fr, 0.0) - _log1p_exp_neg_abs(fr)
    lower = log1m_lb + log_sig
    g = jnp.maximum(log_lb, lower) + _log1p_exp_neg_abs(log_lb - lower)
    kk = (1.0 - lb) * jax.nn.sigmoid(-fr)
    return kk, _cumsum_rows_mxu(g * LOG2E, tril)


def _hgrn_chunk(q, kk, b, v, st, pair_level):
    c = q.shape[0]
    nt = (((1,), (1,)), ((), ()))
    tn = (((0,), (0,)), ((), ()))
    b_last = b[c - 1:c, :]

    o = lax.dot_general((q * jnp.exp2(b)).astype(BF16), st.astype(BF16), nt,
                        preferred_element_type=F32)

    n = q.shape[1]
    a = jnp.where(pair_level == -1,
                  lax.dot_general(q.astype(BF16), kk.astype(BF16), nt, preferred_element_type=F32), 0.0)
    b3 = b.reshape(c // SUBLANES, SUBLANES, n)
    sub3 = lax.broadcasted_iota(jnp.int32, b3.shape, 1)
    level, m = 0, 1
    while m < c:
        if m >= SUBLANES:
            mid = jnp.concatenate(
                [jnp.broadcast_to(b[blk * 2 * m + m - 1:blk * 2 * m + m, :], (2 * m, n))
                 for blk in range(c // (2 * m))], axis=0)
        else:
            mid3 = jnp.broadcast_to(b3[:, m - 1:m, :], b3.shape)
            for first in range(2 * m, SUBLANES, 2 * m):
                mid3 = jnp.where(sub3 >= first,
                                 jnp.broadcast_to(b3[:, first + m - 1:first + m, :], b3.shape), mid3)
            mid = mid3.reshape(c, n)
        e = jnp.exp2(-jnp.abs(b - mid))
        part = lax.dot_general((q * e).astype(BF16), (kk * e).astype(BF16), nt,
                               preferred_element_type=F32)
        a = jnp.where(pair_level == level, part, a)
        level, m = level + 1, 2 * m
    o = o + jnp.dot(a.astype(BF16), v.astype(BF16), preferred_element_type=F32)

    kd = kk * jnp.exp2(b_last - b)
    st_new = st * jnp.exp2(b_last) + lax.dot_general(v.astype(BF16), kd.astype(BF16), tn,
                                                    preferred_element_type=F32)
    return o, st_new


def _hgrn_kernel(q_ref, f_ref, i_ref, g_ref, lb_ref, llb_ref, l1m_ref, nw_ref, o_ref,
                 st_ref, kk_ref, b_ref):
    @pl.when(pl.program_id(2) == 0)
    def _():
        st_ref[...] = jnp.zeros_like(st_ref)

    rows = q_ref.shape[1]
    ri = lax.broadcasted_iota(jnp.int32, (HGRN_CHUNK, HGRN_CHUNK), 0)
    ci_ = lax.broadcasted_iota(jnp.int32, (HGRN_CHUNK, HGRN_CHUNK), 1)
    tril = (ri >= ci_).astype(BF16)
    pair_level = _hgrn_pair_level(HGRN_CHUNK)

    def gates(ci, carry):
        sl = pl.ds(pl.multiple_of(ci * HGRN_CHUNK, HGRN_CHUNK), HGRN_CHUNK)
        kk, b = _hgrn_gates(f_ref[0, sl, :], lb_ref[...], llb_ref[...], l1m_ref[...], tril)
        kk_ref[sl, :] = kk
        b_ref[sl, :] = b
        return carry

    lax.fori_loop(0, rows // HGRN_CHUNK, gates, 0, unroll=True)

    def body(ci, carry):
        r0 = pl.multiple_of(ci * HGRN_CHUNK, HGRN_CHUNK)
        sl = pl.ds(r0, HGRN_CHUNK)
        for hh in range(HGRN_HEADS_PER_STEP):
            hs = slice(hh * HGRN_DIM, (hh + 1) * HGRN_DIM)
            o, st = _hgrn_chunk(q_ref[0, sl, hs], kk_ref[sl, hs], b_ref[sl, hs], i_ref[0, sl, hs],
                                st_ref[hh], pair_level)
            st_ref[hh] = st
            o = o * lax.rsqrt(jnp.mean(o * o, axis=-1, keepdims=True) + NORM_EPS) * nw_ref[:, hs]
            g = g_ref[0, sl, hs]
            o_ref[0, sl, hs] = (o * _silu(g)).astype(o_ref.dtype)
        return carry

    lax.fori_loop(0, rows // HGRN_CHUNK, body, 0, unroll=True)


def _hgrn(proj3, lb, norm_w):
    b, s, _ = proj3.shape
    lb = lb.reshape(1, HGRN_WIDTH)
    log_lb = jnp.log(lb)
    log1m_lb = jnp.log1p(-lb)
    width = HGRN_HEADS_PER_STEP * HGRN_DIM
    blk = lambda c0: pl.BlockSpec((1, HGRN_ROWS, width), lambda bi, h, t: (bi, t, c0 // width + h))
    par = pl.BlockSpec((1, width), lambda bi, h, t: (0, h))
    return pl.pallas_call(
        _hgrn_kernel,
        grid=(b, HGRN_HEADS // HGRN_HEADS_PER_STEP, s // HGRN_ROWS),
        in_specs=[blk(COL_HQ), blk(COL_HF), blk(COL_HI), blk(COL_HG), par, par, par, par],
        out_specs=pl.BlockSpec((1, HGRN_ROWS, width), lambda bi, h, t: (bi, t, h)),
        out_shape=jax.ShapeDtypeStruct((b, s, HGRN_WIDTH), BF16),
        scratch_shapes=[pltpu.VMEM((HGRN_HEADS_PER_STEP, HGRN_DIM, HGRN_DIM), F32),
                        pltpu.VMEM((HGRN_ROWS, width), F32),
                        pltpu.VMEM((HGRN_ROWS, width), F32)],
        compiler_params=_params("parallel", "parallel", "arbitrary"),
        name="hgrn2",
    )(proj3, proj3, proj3, proj3, lb, log_lb, log1m_lb, norm_w.reshape(1, HGRN_WIDTH))


def _softplus(x):
    return jnp.maximum(x, 0.0) + _log1p_exp_neg_abs(x)


def _cumsum_lanes(x):
    n = x.shape[1]
    lane = lax.broadcasted_iota(jnp.int32, x.shape, 1)
    k = 1
    while k < n:
        x = x + jnp.where(lane >= k, pltpu.roll(x, k, axis=1), 0.0)
        k *= 2
    return x


def _ssd_dt_kernel(dt_ref, bias_ref, alog_ref, acrow_ref, accol_ref, dtcol_ref):
    heads, lc = SSM_HEADS, SSD_CHUNK
    neg_a = jnp.exp(alog_ref[...])
    zeros = jnp.zeros((LANES - 3 * heads, lc), F32)
    for c in range(dt_ref.shape[1] // lc):
        sl = slice(c * lc, (c + 1) * lc)
        raw = dt_ref[0, sl, :].T[:heads]
        dt = _softplus(raw + bias_ref[...])
        ac = _cumsum_lanes(dt * (-LOG2E * neg_a))
        acrow_ref[0, :, sl] = ac
        for val, dst in ((ac, accol_ref), (dt, dtcol_ref)):
            pieces = [p.astype(F32) for p in _split3(val)]
            dst[0, sl, :] = jnp.concatenate(pieces + [zeros], axis=0).T.astype(BF16)


def _ssd_dt(dt3, dt_bias, a_log):
    b, s, _ = dt3.shape
    rows = SSD_DT_ROWS
    col = pl.BlockSpec((1, rows, LANES), lambda bi, t: (bi, t, 0))
    par = pl.BlockSpec((SSM_HEADS, 1), lambda bi, t: (0, 0))
    return pl.pallas_call(
        _ssd_dt_kernel,
        grid=(b, s // rows),
        in_specs=[col, par, par],
        out_specs=[pl.BlockSpec((1, SSM_HEADS, rows), lambda bi, t: (bi, 0, t)), col, col],
        out_shape=[jax.ShapeDtypeStruct((b, SSM_HEADS, s), F32),
                   jax.ShapeDtypeStruct((b, s, LANES), BF16),
                   jax.ShapeDtypeStruct((b, s, LANES), BF16)],
        compiler_params=_params("parallel", "parallel"),
        name="ssd_dt",
    )(dt3, dt_bias.reshape(SSM_HEADS, 1), a_log.reshape(SSM_HEADS, 1))


SEL_AC_WIDE = SSM_HEADS_PER_GROUP * LANES
SEL_AC_X = SEL_AC_WIDE + SSM_GROUP_WIDTH
SEL_WIDTH = SEL_AC_X + SSM_GROUP_WIDTH


def _ssd_select_matrices():
    sel = np.zeros((SSM_GROUPS, 2 * LANES, SEL_WIDTH), np.float32)
    for g in range(SSM_GROUPS):
        for e in range(SSM_HEADS_PER_GROUP):
            head = g * SSM_HEADS_PER_GROUP + e
            for piece in range(3):
                r = piece * SSM_HEADS + head
                sel[g, r, e * LANES:(e + 1) * LANES] = 1.0
                sel[g, r, SEL_AC_WIDE + e * SSM_HEAD_DIM:SEL_AC_WIDE + (e + 1) * SSM_HEAD_DIM] = 1.0
                sel[g, LANES + r, SEL_AC_X + e * SSM_HEAD_DIM:SEL_AC_X + (e + 1) * SSM_HEAD_DIM] = 1.0
    return sel


def _ssd_kernel(z_ref, x_ref, b_ref, c_ref, acrow_ref, accol_ref, dtcol_ref, sel_ref,
                cw_ref, cb_ref, dsk_ref, nw_ref, o_ref, st_ref, stage_ref, xbc_ref, acr_ref):
    e_heads, p_dim, lc = SSM_HEADS_PER_GROUP, SSM_HEAD_DIM, SSD_CHUNK
    rows = x_ref.shape[1]
    xw, sw = SSM_GROUP_WIDTH, SSM_STATE
    n_slabs = (xw + 2 * sw) // LANES

    @pl.when(pl.program_id(2) == 0)
    def _():
        st_ref[...] = jnp.zeros_like(st_ref)
        stage_ref[:, :SUBLANES, :] = jnp.zeros((n_slabs, SUBLANES, LANES), F32)

    for slab in range(xw // LANES):
        stage_ref[slab, SUBLANES:, :] = x_ref[0, :, slab * LANES:(slab + 1) * LANES]
    stage_ref[xw // LANES, SUBLANES:, :] = b_ref[0]
    stage_ref[xw // LANES + 1, SUBLANES:, :] = c_ref[0]
    for ci in range(rows // lc):
        acr_ref[ci] = acrow_ref[0, 0, :, ci * lc:(ci + 1) * lc]

    li = lax.broadcasted_iota(jnp.int32, (lc, lc), 0)
    si = lax.broadcasted_iota(jnp.int32, (lc, lc), 1)
    causal = li >= si
    lane_head = lax.broadcasted_iota(jnp.int32, (1, xw), 1) // p_dim
    nt = (((1,), (1,)), ((), ()))
    tn = (((0,), (0,)), ((), ()))
    cw = cw_ref[0]
    cbias = cb_ref[0]

    def body(ci, carry):
        r0 = pl.multiple_of(ci * lc, lc)
        sl = pl.ds(r0, lc)
        for slab in range(n_slabs):
            lanes = slice(slab * LANES, (slab + 1) * LANES)
            for parity in range(2):
                acc = cbias[:, lanes]
                for j in range(SSM_CONV):
                    first = r0 + (SUBLANES - (SSM_CONV - 1) + j + parity)
                    tap = stage_ref[slab, pl.ds(first, lc // 2, stride=2), :]
                    acc = acc + tap * cw[j:j + 1, lanes]
                xbc_ref[slab, pl.ds(r0 + parity, lc // 2, stride=2), :] = _silu(acc)
        xs = jnp.concatenate([xbc_ref[slab, sl, :] for slab in range(xw // LANES)], axis=1)
        bm = xbc_ref[xw // LANES, sl, :].astype(BF16)
        cm = xbc_ref[xw // LANES + 1, sl, :].astype(BF16)

        pieces = jnp.concatenate([accol_ref[0, sl, :], dtcol_ref[0, sl, :]], axis=1)
        bc = jnp.dot(pieces, sel_ref[0], preferred_element_type=F32)
        ac_x = bc[:, SEL_AC_WIDE:SEL_AC_X]
        xdt = xs * bc[:, SEL_AC_X:]
        ac_last = ac_x[lc - 1:lc, :]
        ac_r = acr_ref[ci]

        cb = lax.dot_general(cm, bm, nt, preferred_element_type=F32)
        decayed, x_heads = [], []
        for e in range(e_heads):
            dif = bc[:, e * LANES:(e + 1) * LANES] - ac_r[e:e + 1, :]
            decayed.append((cb * jnp.exp2(jnp.where(causal, dif, NEG))).astype(BF16))
            x_heads.append(jnp.where(lane_head == e, xdt, 0.0).astype(BF16))
        y = jnp.dot(jnp.concatenate(decayed, axis=1), jnp.concatenate(x_heads, axis=0),
                    preferred_element_type=F32)
        st = st_ref[...]
        y = y + jnp.dot(cm, st.astype(BF16), preferred_element_type=F32) * jnp.exp2(ac_x)
        xdec = (xdt * jnp.exp2(ac_last - ac_x)).astype(BF16)
        st_ref[...] = st * jnp.exp2(ac_last) + lax.dot_general(bm, xdec, tn,
                                                              preferred_element_type=F32)
        y = y + xs * dsk_ref[...]
        y = y * _silu(z_ref[0, sl, :])
        y = y * lax.rsqrt(jnp.mean(y * y, axis=-1, keepdims=True) + NORM_EPS) * nw_ref[...]
        o_ref[0, sl, :] = y.astype(o_ref.dtype)
        return carry

    lax.fori_loop(0, rows // lc, body, 0, unroll=True)
    stage_ref[:, :SUBLANES, :] = stage_ref[:, rows:rows + SUBLANES, :]


def _ssd(proj3, dt3, conv_w, conv_b, dt_bias, a_log, d_skip, norm_w):
    b, s, _ = proj3.shape
    g, e = SSM_GROUPS, SSM_HEADS_PER_GROUP
    rows = SSD_ROWS
    ac_row, ac_col, dt_col = _ssd_dt(dt3, dt_bias, a_log)
    ac_row = ac_row.reshape(b, g, e, s)
    sel = jnp.asarray(_ssd_select_matrices(), BF16)

    def per_group(p):
        px = p[:, :SSM_WIDTH].reshape(-1, g, SSM_GROUP_WIDTH)
        pb = p[:, SSM_WIDTH:SSM_WIDTH + g * SSM_STATE].reshape(-1, g, SSM_STATE)
        pc = p[:, SSM_WIDTH + g * SSM_STATE:].reshape(-1, g, SSM_STATE)
        return jnp.concatenate([px, pb, pc], axis=2).transpose(1, 0, 2)

    conv_width = SSM_GROUP_WIDTH + 2 * SSM_STATE
    dsk = jnp.repeat(d_skip, SSM_HEAD_DIM).reshape(1, SSM_WIDTH)

    def act(c0, width):
        return pl.BlockSpec((1, rows, width), lambda bi, gi, t: (bi, t, c0 // width + gi))

    col = pl.BlockSpec((1, rows, LANES), lambda bi, gi, t: (bi, t, 0))
    grp = lambda nrows, width: pl.BlockSpec((1, nrows, width), lambda bi, gi, t: (gi, 0, 0))
    par = pl.BlockSpec((1, SSM_GROUP_WIDTH), lambda bi, gi, t: (0, gi))
    return pl.pallas_call(
        _ssd_kernel,
        grid=(b, g, s // rows),
        in_specs=[act(COL_SZ, SSM_GROUP_WIDTH), act(COL_SX, SSM_GROUP_WIDTH),
                  act(COL_SB, SSM_STATE), act(COL_SC, SSM_STATE),
                  pl.BlockSpec((1, 1, e, rows), lambda bi, gi, t: (bi, gi, 0, t)),
                  col, col, grp(2 * LANES, SEL_WIDTH),
                  grp(SSM_CONV, conv_width), grp(1, conv_width), par, par],
        out_specs=pl.BlockSpec((1, rows, SSM_GROUP_WIDTH), lambda bi, gi, t: (bi, t, gi)),
        out_shape=jax.ShapeDtypeStruct((b, s, SSM_WIDTH), BF16),
        scratch_shapes=[pltpu.VMEM((SSM_STATE, SSM_GROUP_WIDTH), F32),
                        pltpu.VMEM((conv_width // LANES, SUBLANES + rows, LANES), F32),
                        pltpu.VMEM((conv_width // LANES, rows, LANES), F32),
                        pltpu.VMEM((rows // SSD_CHUNK, e, SSD_CHUNK), F32)],
        compiler_params=_params("parallel", "parallel", "arbitrary"),
        name="ssd",
    )(proj3, proj3, proj3, proj3, ac_row, ac_col, dt_col, sel,
      per_group(conv_w), per_group(conv_b.reshape(1, -1)), dsk, norm_w.reshape(1, SSM_WIDTH))


def _mixers(h, row_scale, dt, batch, cos, sin, w_in_t, layer, conv_w, conv_b, dt_bias, a_log, d_skip,
            hgrn_norm_w, ssm_norm_w, lb):
    m = h.shape[0]
    s = m // batch
    proj3 = _inproj(h, w_in_t, layer, row_scale).reshape(batch, s, IN_MAIN)
    dt3 = dt.reshape(batch, s, LANES)
    q_rot, k_pad, v_pad = _rope_apply(proj3, cos, sin)
    ya = _attention(q_rot, k_pad, v_pad, proj3)
    yh = _hgrn(proj3, lb, hgrn_norm_w)
    ys = _ssd(proj3, dt3, conv_w, conv_b, dt_bias, a_log, d_skip, ssm_norm_w)
    return ya.reshape(m, -1), yh.reshape(m, -1), ys.reshape(m, -1)


def kernel(x, positions, norm_w, w_in, conv_w, conv_b, dt_bias, a_log, d_skip, hgrn_norm_w,
           ssm_norm_w, w_out, hgrn_lb_logits, final_norm_w):
    batch, s, d = x.shape
    p = jax.nn.softmax(hgrn_lb_logits.astype(F32), axis=0)
    cs = jnp.cumsum(p, axis=0)
    lb_all = cs - cs[0:1]
    w_dt = jnp.pad(w_in[:, :, IN_MAIN:], ((0, 0), (0, 0), (0, LANES - SSM_HEADS))).astype(BF16)
    w_in_t = jnp.swapaxes(w_in, 1, 2)
    w_out_bf = w_out.astype(BF16)
    cos, sin = _rope_tables(positions)
    x2 = x.reshape(batch * s, d)
    h, dt = _rmsnorm_dt(x2, norm_w[0], w_dt[0])
    row_scale = None
    for l in range(DEPTH):
        ya, yh, ys = _mixers(h, row_scale, dt, batch, cos, sin, w_in_t, l, conv_w[l], conv_b[l],
                             dt_bias[l], a_log[l], d_skip[l], hgrn_norm_w[l], ssm_norm_w[l], lb_all[l])
        if l + 1 < DEPTH:
            x2, h, row_scale, dt = _outproj_mid(ya, yh, ys, w_out_bf, l, x2, norm_w[l + 1], w_dt[l + 1])
        else:
            out = _outproj_final(ya, yh, ys, w_out_bf, l, x2, final_norm_w)
    return out.reshape(batch, s, d)
```
